```python
import math
import jax, jax.numpy as jnp
from jax import lax
import numpy as np

D_MODEL = 2048
BATCH = 4
SEQ = 4096
DEPTH = 1

HEAD_DIM = 64
N_HEADS = D_MODEL // 256
ATTN_WIDTH = N_HEADS * 2 * HEAD_DIM
POOL_WINDOWS = (2, 4, 8, 16)
N_POOL_GROUPS = len(POOL_WINDOWS)
POOL_WIDTH = D_MODEL // 2
POOL_GROUP_DIM = POOL_WIDTH // N_POOL_GROUPS
N_BRANCH = 2
IN_COLS = 3 * ATTN_WIDTH + POOL_WIDTH + N_BRANCH * D_MODEL
D_FF = (11 * D_MODEL // 4) // 128 * 128
CONV_WIDTH = 3
Q_BLOCK = 128
NORM_EPS = 1e-6
SUBLN_EPS = 1e-5

kernel_name = 'hybrid_diffattn_pool_convffn_block'


def rms_norm(x, g, eps=NORM_EPS):
    xf = x.astype(jnp.float32)
    y = xf * lax.rsqrt(jnp.mean(xf * xf, axis=-1, keepdims=True) + eps)
    return (y * g.astype(jnp.float32)).astype(x.dtype)


def alibi_slopes():
    return jnp.asarray(np.array([2.0 ** (-8.0 * (h + 1) / N_HEADS) for h in range(N_HEADS)], dtype=np.float32))


def diff_attention(q, k, v, lam, slopes):
    B, S = q.shape[0], q.shape[1]
    nb = S // Q_BLOCK
    q_blocks = q.reshape(B, nb, Q_BLOCK, N_HEADS, 2, HEAD_DIM).transpose(1, 0, 2, 3, 4, 5)
    starts = jnp.arange(nb, dtype=jnp.int32) * Q_BLOCK
    key_pos = jnp.arange(S, dtype=jnp.int32)
    scale = HEAD_DIM ** -0.5

    def one_block(args):
        qb, start = args
        s = jnp.einsum('bqhcd,bkhcd->bhcqk', qb, k).astype(jnp.float32) * scale
        q_pos = start + jnp.arange(Q_BLOCK, dtype=jnp.int32)
        dist = jnp.abs(q_pos[:, None] - key_pos[None, :]).astype(jnp.float32)
        s = s - slopes[None, :, None, None, None] * dist[None, None, None]
        p = jax.nn.softmax(s, axis=-1)
        a = p[:, :, 0] - lam * p[:, :, 1]
        return jnp.einsum('bhqk,bkhe->bqhe', a.astype(v.dtype), v)

    out = lax.map(one_block, (q_blocks, starts))
    return out.transpose(1, 0, 2, 3, 4).reshape(B, S, N_HEADS, 2 * HEAD_DIM)


def pool_mixer(p, pool_w, pool_scale):
    B, S, _ = p.shape
    pg = p.reshape(B, S, N_POOL_GROUPS, POOL_GROUP_DIM)
    pf = pg.astype(jnp.float32)
    csum = jnp.pad(jnp.cumsum(pf, axis=1), ((0, 0), (1, 0), (0, 0), (0, 0)))
    t = jnp.arange(S, dtype=jnp.int32)
    halves = jnp.asarray(np.array([w // 2 for w in POOL_WINDOWS], dtype=np.int32))
    lo = jnp.clip(t[None, :] - halves[:, None], 0, S)
    hi = jnp.clip(t[None, :] + halves[:, None], 0, S)
    cnt = (hi - lo).astype(jnp.float32)
    c_g = csum.transpose(2, 0, 1, 3)
    take = jax.vmap(lambda cg, idx: cg[:, idx])
    win_sum = take(c_g, hi) - take(c_g, lo)
    mean = win_sum / cnt[:, None, :, None]
    mixed = (mean.transpose(1, 2, 0, 3) - pf).astype(p.dtype)
    y = jnp.einsum('bsgc,gcd->bsgd', mixed, pool_w)
    y = y * pool_scale.reshape(N_POOL_GROUPS, POOL_GROUP_DIM)
    return y.reshape(B, S, POOL_WIDTH)


def depthwise_conv_centred(u, w, b):
    up = jnp.pad(u, ((0, 0), (1, 1), (0, 0)))
    return up[:, :-2] * w[0] + up[:, 1:-1] * w[1] + up[:, 2:] * w[2] + b


def setup_inputs(seed: int = 0) -> dict:
    key = jax.random.key(seed)
    ks = jax.random.split(key, 24)
    f = jnp.float32
    L = DEPTH

    def nrm(k, shape, s):
        return jax.random.normal(k, shape, f) * s

    return {
        'x': nrm(ks[0], (BATCH, SEQ, D_MODEL), 1.0),
        'g_pre_mix': 1.0 + nrm(ks[1], (L, D_MODEL), 0.02),
        'w_in': nrm(ks[2], (L, D_MODEL, IN_COLS), D_MODEL ** -0.5),
        'b_gate': nrm(ks[3], (L, N_BRANCH * D_MODEL), 0.02),
        'lambda_q1': nrm(ks[4], (L, HEAD_DIM), 0.1),
        'lambda_k1': nrm(ks[5], (L, HEAD_DIM), 0.1),
        'lambda_q2': nrm(ks[6], (L, HEAD_DIM), 0.1),
        'lambda_k2': nrm(ks[7], (L, HEAD_DIM), 0.1),
        'subln_gain': 1.0 + nrm(ks[8], (L, 2 * HEAD_DIM), 0.02),
        'w_attn_branch': nrm(ks[9], (L, ATTN_WIDTH, D_MODEL), ATTN_WIDTH ** -0.5),
        'pool_w': nrm(ks[10], (L, N_POOL_GROUPS, POOL_GROUP_DIM, POOL_GROUP_DIM), POOL_GROUP_DIM ** -0.5),
        'pool_scale': 1.0 + nrm(ks[11], (L, POOL_WIDTH), 0.02),
        'w_pool_branch': nrm(ks[12], (L, POOL_WIDTH, D_MODEL), POOL_WIDTH ** -0.5),
        'w_out': nrm(ks[13], (L, D_MODEL, D_MODEL), D_MODEL ** -0.5),
        'g_post_mix': 1.0 + nrm(ks[14], (L, D_MODEL), 0.02),
        'g_pre_ffn': 1.0 + nrm(ks[15], (L, D_MODEL), 0.02),
        'w_up': nrm(ks[16], (L, D_MODEL, 2 * D_FF), D_MODEL ** -0.5),
        'conv_w': nrm(ks[17], (L, CONV_WIDTH, 2 * D_FF), CONV_WIDTH ** -0.5),
        'conv_b': nrm(ks[18], (L, 2 * D_FF), 0.02),
        'w_down': nrm(ks[19], (L, D_FF, D_MODEL), D_FF ** -0.5),
        'g_post_ffn': 1.0 + nrm(ks[20], (L, D_MODEL), 0.02),
    }


def reference(x, g_pre_mix, w_in, b_gate, lambda_q1, lambda_k1, lambda_q2, lambda_k2, subln_gain,
              w_attn_branch, pool_w, pool_scale, w_pool_branch, w_out, g_post_mix,
              g_pre_ffn, w_up, conv_w, conv_b, w_down, g_post_ffn):
    B, S, D = x.shape
    slopes = alibi_slopes()
    splits = [ATTN_WIDTH, 2 * ATTN_WIDTH, 3 * ATTN_WIDTH, 3 * ATTN_WIDTH + POOL_WIDTH]
    for l in range(DEPTH):
        lambda_init = 0.8 - 0.6 * math.exp(-0.3 * l)
        h = rms_norm(x, g_pre_mix[l])
        z = h @ w_in[l]
        q, k, v, pz, gz = jnp.split(z, splits, axis=-1)
        q = q.reshape(B, S, N_HEADS, 2, HEAD_DIM)
        k = k.reshape(B, S, N_HEADS, 2, HEAD_DIM)
        v = v.reshape(B, S, N_HEADS, 2 * HEAD_DIM)
        lam = (jnp.exp(jnp.sum(lambda_q1[l].astype(jnp.float32) * lambda_k1[l].astype(jnp.float32)))
               - jnp.exp(jnp.sum(lambda_q2[l].astype(jnp.float32) * lambda_k2[l].astype(jnp.float32)))
               + lambda_init)
        o = diff_attention(q, k, v, lam, slopes)
        o = rms_norm(o, subln_gain[l], SUBLN_EPS) * (1.0 - lambda_init)
        y_attn = o.reshape(B, S, ATTN_WIDTH) @ w_attn_branch[l]
        y_pool = pool_mixer(pz, pool_w[l], pool_scale[l]) @ w_pool_branch[l]
        gates = jax.nn.sigmoid(gz + b_gate[l]).reshape(B, S, N_BRANCH, D)
        merged = gates[:, :, 0] * y_attn + gates[:, :, 1] * y_pool
        x = x + rms_norm(merged @ w_out[l], g_post_mix[l])
        h2 = rms_norm(x, g_pre_ffn[l])
        u = depthwise_conv_centred(h2 @ w_up[l], conv_w[l], conv_b[l])
        gate, val = jnp.split(u, 2, axis=-1)
        y = (jax.nn.gelu(gate) * val) @ w_down[l]
        x = x + rms_norm(y, g_post_ffn[l])
    return x
```

```python
import functools
import math

import jax
import jax.numpy as jnp
import numpy as np
from jax import lax
from jax.experimental import pallas as pl
from jax.experimental.pallas import tpu as pltpu

HEAD_DIM = 64
HEAD_WIDTH = 2 * HEAD_DIM
POOL_WINDOWS = (2, 4, 8, 16)
CONV_WIDTH = 3
NORM_EPS = 1e-6
SUBLN_EPS = 1e-5
HALO = 16
V7X_VMEM_LIMIT_BYTES = 56 * 1024 * 1024

_F32 = jnp.float32
_BF16 = jnp.bfloat16


def _rms(x, g, eps):
    return x * lax.rsqrt(jnp.mean(x * x, axis=-1, keepdims=True) + eps) * g


def _params(*sem):
    return pltpu.CompilerParams(dimension_semantics=sem, vmem_limit_bytes=V7X_VMEM_LIMIT_BYTES)


def _in_proj_kernel(x_ref, g_ref, w_ref, qkv_ref, pz_ref, gz_ref, h_ref, *, n_heads):
    j = pl.program_id(1)

    @pl.when(j == 0)
    def _():
        h_ref[...] = _rms(x_ref[...], g_ref[...], NORM_EPS).astype(_BF16)

    z = jnp.dot(h_ref[...], w_ref[...], preferred_element_type=_F32).astype(_BF16)

    @pl.when(j < 3)
    def _():
        for hh in range(n_heads):
            qkv_ref[0, hh] = z[:, hh * HEAD_WIDTH:(hh + 1) * HEAD_WIDTH]

    @pl.when(j == 3)
    def _():
        pz_ref[...] = z

    @pl.when(j > 3)
    def _():
        gz_ref[...] = z


def _in_proj(x2, g, w, *, n_heads, tm):
    T, D = x2.shape
    aw = n_heads * HEAD_WIDTH
    n_gate_blocks = 2 * D // aw
    grid = (T // tm, 4 + n_gate_blocks)
    return pl.pallas_call(
        functools.partial(_in_proj_kernel, n_heads=n_heads),
        grid=grid,
        in_specs=[
            pl.BlockSpec((tm, D), lambda i, j: (i, 0)),
            pl.BlockSpec((1, D), lambda i, j: (0, 0)),
            pl.BlockSpec((D, aw), lambda i, j: (0, j)),
        ],
        out_specs=[
            pl.BlockSpec((1, n_heads, tm, HEAD_WIDTH), lambda i, j: (jnp.minimum(j, 2), 0, i, 0)),
            pl.BlockSpec((tm, aw), lambda i, j: (i, 0)),
            pl.BlockSpec((tm, aw), lambda i, j: (i, jnp.maximum(j - 4, 0))),
        ],
        out_shape=[
            jax.ShapeDtypeStruct((3, n_heads, T, HEAD_WIDTH), _BF16),
            jax.ShapeDtypeStruct((T, aw), _BF16),
            jax.ShapeDtypeStruct((T, 2 * D), _BF16),
        ],
        scratch_shapes=[pltpu.VMEM((tm, D), _BF16)],
        compiler_params=_params("arbitrary", "arbitrary"),
        name="in_proj",
    )(x2, g, w)


def _attn_kernel(slopes_ref, lam_ref, gain_ref, q_ref, k_ref, v_ref, o_ref,
                 ksplit_ref, vt_ref, acc_ref, *, tq, tk, n_kb, lambda_init):
    h = pl.program_id(1)
    qi = pl.program_id(2)
    slope = slopes_ref[h]

    @pl.when(qi == 0)
    def _():
        lane = lax.broadcasted_iota(jnp.int32, (tk, HEAD_WIDTH), 1)
        for jb in range(n_kb):
            kb = k_ref[0, 0, jb * tk:(jb + 1) * tk, :]
            ksplit_ref[0, jb] = jnp.where(lane < HEAD_DIM, kb, jnp.zeros_like(kb))
            ksplit_ref[1, jb] = jnp.where(lane >= HEAD_DIM, kb, jnp.zeros_like(kb))
            vb = v_ref[0, 0, jb * tk:(jb + 1) * tk, :].astype(_F32)
            vt_ref[jb] = vb.T.astype(_BF16)

    scale = HEAD_DIM ** -0.5
    q_t = (q_ref[0, 0].astype(_F32) * scale).T.astype(_BF16)
    rel = (lax.broadcasted_iota(jnp.int32, (tk, tq), 0)
           - lax.broadcasted_iota(jnp.int32, (tk, tq), 1)).astype(_F32)
    acc_ref[...] = jnp.zeros_like(acc_ref)

    def body(jb, carry):
        m0, l0, m1, l1 = carry
        offset = (jb * tk - qi * tq).astype(_F32)
        bias = slope * jnp.abs(rel + offset)
        v_t = vt_ref[jb]
        new = []
        for c, (m, l) in enumerate(((m0, l0), (m1, l1))):
            s = jnp.dot(ksplit_ref[c, jb], q_t, preferred_element_type=_F32) - bias
            m_new = jnp.maximum(m, jnp.max(s, axis=0, keepdims=True))
            alpha = jnp.exp(m - m_new)
            p = jnp.exp(s - m_new)
            l_new = alpha * l + jnp.sum(p, axis=0, keepdims=True)
            acc_ref[c] = alpha * acc_ref[c] + jnp.dot(v_t, p.astype(_BF16),
                                                      preferred_element_type=_F32)
            new += [m_new, l_new]
        return tuple(new)

    m_init = jnp.full((1, tq), -1e30, _F32)
    l_init = jnp.zeros((1, tq), _F32)
    _, l0, _, l1 = lax.fori_loop(0, n_kb, body, (m_init, l_init, m_init, l_init))

    lam_p = lam_ref[...]
    lam = (jnp.exp(jnp.sum(lam_p[0:1] * lam_p[1:2], axis=-1, keepdims=True))
           - jnp.exp(jnp.sum(lam_p[2:3] * lam_p[3:4], axis=-1, keepdims=True))
           + lambda_init)
    o_t = acc_ref[0] / l0 - lam * (acc_ref[1] / l1)
    ms = jnp.mean(o_t * o_t, axis=0, keepdims=True)
    o_t = o_t * lax.rsqrt(ms + SUBLN_EPS) * gain_ref[...] * (1.0 - lambda_init)
    o_ref[...] = o_t.T.astype(o_ref.dtype)


def _attention(qkv, slopes, lam_params, gain_col, *, batch, seq, tq, tk, lambda_init):
    _, n_heads, T, _ = qkv.shape
    n_q = seq // tq
    n_kb = seq // tk
    kern = functools.partial(_attn_kernel, tq=tq, tk=tk, n_kb=n_kb, lambda_init=lambda_init)
    return pl.pallas_call(
        kern,
        grid=(batch, n_heads, n_q),
        in_specs=[
            pl.BlockSpec(memory_space=pltpu.SMEM),
            pl.BlockSpec((4, HEAD_DIM), lambda b, h, i: (0, 0)),
            pl.BlockSpec((HEAD_WIDTH, 1), lambda b, h, i: (0, 0)),
            pl.BlockSpec((1, 1, tq, HEAD_WIDTH), lambda b, h, i: (0, h, b * n_q + i, 0)),
            pl.BlockSpec((1, 1, seq, HEAD_WIDTH), lambda b, h, i: (1, h, b, 0)),
            pl.BlockSpec((1, 1, seq, HEAD_WIDTH), lambda b, h, i: (2, h, b, 0)),
        ],
        out_specs=pl.BlockSpec((tq, HEAD_WIDTH), lambda b, h, i: (b * n_q + i, h)),
        out_shape=jax.ShapeDtypeStruct((T, n_heads * HEAD_WIDTH), _BF16),
        scratch_shapes=[
            pltpu.VMEM((2, n_kb, tk, HEAD_WIDTH), _BF16),
            pltpu.VMEM((n_kb, HEAD_WIDTH, tk), _BF16),
            pltpu.VMEM((2, HEAD_WIDTH, tq), _F32),
        ],
        compiler_params=_params("arbitrary", "arbitrary", "arbitrary"),
        name="diff_attn",
    )(slopes, lam_params, gain_col, qkv, qkv, qkv)


def _merge_kernel(o_ref, pz_ref, pzp_ref, pzn_ref, gz_ref, x_ref, bg_ref, pw_ref, ps_ref,
                  wa_ref, wp_ref, wo_ref, g_ref, out_ref, ext_ref, yp_ref, *, tm, seq):
    D = x_ref.shape[1]
    gdim = pw_ref.shape[1]
    t0 = (pl.program_id(0) * tm) % seq
    prev = pzp_ref[...].astype(_F32)
    nxt = pzn_ref[...].astype(_F32)
    ext_ref[0:HALO] = jnp.where(t0 == 0, jnp.zeros_like(prev), prev)
    ext_ref[HALO:HALO + tm] = pz_ref[...].astype(_F32)
    ext_ref[HALO + tm:2 * HALO + tm] = jnp.where(t0 + tm == seq, jnp.zeros_like(nxt), nxt)

    tpos = t0 + lax.broadcasted_iota(jnp.int32, (tm, 1), 0)
    for g, w in enumerate(POOL_WINDOWS):
        half = w // 2
        cols = slice(g * gdim, (g + 1) * gdim)
        win = ext_ref[pl.ds(HALO - half, tm), cols]
        for k in range(1, w):
            win = win + ext_ref[pl.ds(HALO - half + k, tm), cols]
        cnt = (jnp.minimum(tpos + half, seq) - jnp.maximum(tpos - half, 0)).astype(_F32)
        mixed = win / cnt - ext_ref[HALO:HALO + tm, cols]
        yg = jnp.dot(mixed.astype(_BF16), pw_ref[g], preferred_element_type=_F32)
        yp_ref[:, cols] = (yg * ps_ref[:, cols]).astype(_BF16)

    y_attn = jnp.dot(o_ref[...], wa_ref[...], preferred_element_type=_F32)
    y_pool = jnp.dot(yp_ref[...], wp_ref[...], preferred_element_type=_F32)
    gates = jax.nn.sigmoid(gz_ref[...].astype(_F32) + bg_ref[...])
    merged = gates[:, :D] * y_attn + gates[:, D:] * y_pool
    mo = jnp.dot(merged.astype(_BF16), wo_ref[...], preferred_element_type=_F32)
    out_ref[...] = x_ref[...] + _rms(mo, g_ref[...], NORM_EPS)


def _const_spec(shape):
    nd = len(shape)
    return pl.BlockSpec(shape, lambda i: (0,) * nd, pipeline_mode=pl.Buffered(1))


def _merge(o, pz, gz, x2, b_gate, pool_w, pool_scale, w_attn, w_pool, w_out, g_post, *, seq, tm):
    T, D = x2.shape
    aw = o.shape[1]
    hb = tm // HALO
    n_hb = T // HALO
    return pl.pallas_call(
        functools.partial(_merge_kernel, tm=tm, seq=seq),
        grid=(T // tm,),
        in_specs=[
            pl.BlockSpec((tm, aw), lambda i: (i, 0)),
            pl.BlockSpec((tm, aw), lambda i: (i, 0)),
            pl.BlockSpec((HALO, aw), lambda i: (jnp.maximum(i * hb - 1, 0), 0)),
            pl.BlockSpec((HALO, aw), lambda i: (jnp.minimum((i + 1) * hb, n_hb - 1), 0)),
            pl.BlockSpec((tm, 2 * D), lambda i: (i, 0)),
            pl.BlockSpec((tm, D), lambda i: (i, 0)),
            _const_spec(b_gate.shape),
            _const_spec(pool_w.shape),
            _const_spec(pool_scale.shape),
            _const_spec(w_attn.shape),
            _const_spec(w_pool.shape),
            _const_spec(w_out.shape),
            _const_spec(g_post.shape),
        ],
        out_specs=pl.BlockSpec((tm, D), lambda i: (i, 0)),
        out_shape=jax.ShapeDtypeStruct((T, D), _F32),
        scratch_shapes=[
            pltpu.VMEM((tm + 2 * HALO, aw), _F32),
            pltpu.VMEM((tm, aw), _BF16),
        ],
        compiler_params=_params("arbitrary"),
        name="merge",
    )(o, pz, pz, pz, gz, x2, b_gate, pool_w, pool_scale, w_attn, w_pool, w_out, g_post)


def _ffn_kernel(x_ref, xp_ref, xn_ref, g_ref, wg_ref, wv_ref, cwg_ref, cwv_ref, cbg_ref, cbv_ref,
                wd_ref, gp_ref, out_ref, h_ref, acc_ref, *, tm, seq):
    c = pl.program_id(1)
    rows = tm + 2 * HALO

    @pl.when(c == 0)
    def _():
        t0 = (pl.program_id(0) * tm) % seq
        g = g_ref[...]
        hp = _rms(xp_ref[...], g, NORM_EPS)
        hn = _rms(xn_ref[...], g, NORM_EPS)
        h_ref[0:HALO] = jnp.where(t0 == 0, jnp.zeros_like(hp), hp).astype(_BF16)
        h_ref[HALO:HALO + tm] = _rms(x_ref[...], g, NORM_EPS).astype(_BF16)
        h_ref[HALO + tm:rows] = jnp.where(t0 + tm == seq, jnp.zeros_like(hn), hn).astype(_BF16)
        acc_ref[...] = jnp.zeros_like(acc_ref)

    h = h_ref[...]

    def conv(w_ref, cw_ref, cb_ref):
        u = jnp.dot(h, w_ref[...], preferred_element_type=_F32)
        cw = cw_ref[...]
        u_prev = pltpu.roll(u, 1, axis=0)[HALO:HALO + tm]
        u_next = pltpu.roll(u, rows - 1, axis=0)[HALO:HALO + tm]
        return u_prev * cw[0:1] + u[HALO:HALO + tm] * cw[1:2] + u_next * cw[2:3] + cb_ref[...]

    gate = conv(wg_ref, cwg_ref, cbg_ref)
    val = conv(wv_ref, cwv_ref, cbv_ref)
    act = (jax.nn.gelu(gate) * val).astype(_BF16)
    acc_ref[...] += jnp.dot(act, wd_ref[...], preferred_element_type=_F32)

    @pl.when(c == pl.num_programs(1) - 1)
    def _():
        out_ref[...] = x_ref[...] + _rms(acc_ref[...], gp_ref[...], NORM_EPS)


def _ffn(x1, g_pre, w_up, conv_w, conv_b, w_down, g_post, *, seq, tm, tf):
    T, D = x1.shape
    d_ff = w_down.shape[0]
    n_c = d_ff // tf
    hb = tm // HALO
    n_hb = T // HALO
    return pl.pallas_call(
        functools.partial(_ffn_kernel, tm=tm, seq=seq),
        grid=(T // tm, n_c),
        in_specs=[
            pl.BlockSpec((tm, D), lambda i, c: (i, 0)),
            pl.BlockSpec((HALO, D), lambda i, c: (jnp.maximum(i * hb - 1, 0), 0)),
            pl.BlockSpec((HALO, D), lambda i, c: (jnp.minimum((i + 1) * hb, n_hb - 1), 0)),
            pl.BlockSpec((1, D), lambda i, c: (0, 0)),
            pl.BlockSpec((D, tf), lambda i, c: (0, c)),
            pl.BlockSpec((D, tf), lambda i, c: (0, n_c + c)),
            pl.BlockSpec((CONV_WIDTH, tf), lambda i, c: (0, c)),
            pl.BlockSpec((CONV_WIDTH, tf), lambda i, c: (0, n_c + c)),
            pl.BlockSpec((1, tf), lambda i, c: (0, c)),
            pl.BlockSpec((1, tf), lambda i, c: (0, n_c + c)),
            pl.BlockSpec((tf, D), lambda i, c: (c, 0)),
            pl.BlockSpec((1, D), lambda i, c: (0, 0)),
        ],
        out_specs=pl.BlockSpec((tm, D), lambda i, c: (i, 0)),
        out_shape=jax.ShapeDtypeStruct((T, D), _F32),
        scratch_shapes=[
            pltpu.VMEM((tm + 2 * HALO, D), _BF16),
            pltpu.VMEM((tm, D), _F32),
        ],
        compiler_params=_params("arbitrary", "arbitrary"),
        name="conv_ffn",
    )(x1, x1, x1, g_pre, w_up, w_up, conv_w, conv_w, conv_b, conv_b, w_down, g_post)


def _pick(n, pref):
    t = min(n, pref)
    while n % t:
        t //= 2
    return t


def kernel(x, g_pre_mix, w_in, b_gate, lambda_q1, lambda_k1, lambda_q2, lambda_k2, subln_gain,
           w_attn_branch, pool_w, pool_scale, w_pool_branch, w_out, g_post_mix, g_pre_ffn, w_up,
           conv_w, conv_b, w_down, g_post_ffn):
    B, S, D = x.shape
    depth = w_in.shape[0]
    aw = w_attn_branch.shape[1]
    n_heads = aw // HEAD_WIDTH
    d_ff = w_down.shape[1]
    T = B * S
    slopes = jnp.asarray(
        np.array([2.0 ** (-8.0 * (h + 1) / n_heads) for h in range(n_heads)], dtype=np.float32))
    row = lambda a: a.reshape(1, -1).astype(_F32)

    x2 = x.reshape(T, D)
    for l in range(depth):
        lambda_init = 0.8 - 0.6 * math.exp(-0.3 * l)
        qkv, pz, gz = _in_proj(x2, row(g_pre_mix[l]), w_in[l].astype(_BF16),
                               n_heads=n_heads, tm=_pick(T, 1024))
        lam_params = jnp.stack([lambda_q1[l], lambda_k1[l], lambda_q2[l], lambda_k2[l]]).astype(_F32)
        o = _attention(qkv, slopes, lam_params, subln_gain[l].reshape(-1, 1).astype(_F32),
                       batch=B, seq=S, tq=_pick(S, 256), tk=_pick(S, 512), lambda_init=lambda_init)
        x2 = _merge(o, pz, gz, x2, row(b_gate[l]), pool_w[l].astype(_BF16), row(pool_scale[l]),
                    w_attn_branch[l].astype(_BF16), w_pool_branch[l].astype(_BF16),
                    w_out[l].astype(_BF16), row(g_post_mix[l]), seq=S, tm=_pick(S, 256))
        x2 = _ffn(x2, row(g_pre_ffn[l]), w_up[l].astype(_BF16), conv_w[l].astype(_F32),
                  row(conv_b[l]), w_down[l].astype(_BF16), row(g_post_ffn[l]),
                  seq=S, tm=_pick(S, 512), tf=_pick(d_ff, 512))
    return x2.reshape(B, S, D)
```

```python
import functools
import math

import jax
import jax.numpy as jnp
import numpy as np
from jax import lax
from jax.experimental import pallas as pl
from jax.experimental.pallas import tpu as pltpu

HEAD_DIM = 64
HEAD_WIDTH = 2 * HEAD_DIM
POOL_WINDOWS = (2, 4, 8, 16)
CONV_WIDTH = 3
NORM_EPS = 1e-6
SUBLN_EPS = 1e-5
HALO = 16
F32_SUBLANES = 8
POS_RADIX = 64
LOG2E = math.log2(math.e)
V7X_VMEM_LIMIT_BYTES = 56 * 1024 * 1024

_F32 = jnp.float32
_BF16 = jnp.bfloat16


def _rms(x, g, eps):
    return x * lax.rsqrt(jnp.mean(x * x, axis=-1, keepdims=True) + eps) * g


def _params(*sem):
    return pltpu.CompilerParams(dimension_semantics=sem, vmem_limit_bytes=V7X_VMEM_LIMIT_BYTES)


def _in_proj_kernel(x_ref, g_ref, w_ref, qkv_ref, pz_ref, gz_ref, h_ref, *, n_heads):
    j = pl.program_id(1)

    @pl.when(j == 0)
    def _():
        h_ref[...] = _rms(x_ref[...], g_ref[...], NORM_EPS).astype(_BF16)

    z = jnp.dot(h_ref[...], w_ref[...], preferred_element_type=_F32).astype(_BF16)

    @pl.when(j < 3)
    def _():
        for hh in range(n_heads):
            qkv_ref[0, hh] = z[:, hh * HEAD_WIDTH:(hh + 1) * HEAD_WIDTH]

    @pl.when(j == 3)
    def _():
        pz_ref[...] = z

    @pl.when(j > 3)
    def _():
        gz_ref[...] = z


def _in_proj(x2, g, w, *, n_heads, tm):
    T, D = x2.shape
    aw = n_heads * HEAD_WIDTH
    n_gate_blocks = 2 * D // aw
    grid = (T // tm, 4 + n_gate_blocks)
    return pl.pallas_call(
        functools.partial(_in_proj_kernel, n_heads=n_heads),
        grid=grid,
        in_specs=[
            pl.BlockSpec((tm, D), lambda i, j: (i, 0)),
            pl.BlockSpec((1, D), lambda i, j: (0, 0)),
            pl.BlockSpec((D, aw), lambda i, j: (0, j)),
        ],
        out_specs=[
            pl.BlockSpec((1, n_heads, tm, HEAD_WIDTH), lambda i, j: (jnp.minimum(j, 2), 0, i, 0)),
            pl.BlockSpec((tm, aw), lambda i, j: (i, 0)),
            pl.BlockSpec((tm, aw), lambda i, j: (i, jnp.maximum(j - 4, 0))),
        ],
        out_shape=[
            jax.ShapeDtypeStruct((3, n_heads, T, HEAD_WIDTH), _BF16),
            jax.ShapeDtypeStruct((T, aw), _BF16),
            jax.ShapeDtypeStruct((T, 2 * D), _BF16),
        ],
        scratch_shapes=[pltpu.VMEM((tm, D), _BF16)],
        compiler_params=_params("arbitrary", "arbitrary"),
        name="in_proj",
    )(x2, g, w)


def _pos_features(pos, idx, base, values):
    hi = (pos // POS_RADIX).astype(_F32)
    lo = (pos % POS_RADIX).astype(_F32)
    out = jnp.zeros(pos.shape, _F32)
    for n, val in enumerate(values(hi, lo)):
        out = jnp.where(idx == base + n, val, out)
    return out


def _attn_kernel(slopes_ref, lam_ref, gain_ref, q_ref, k_ref, v_ref, o_ref,
                 kaug_ref, vt_ref, s_ref, acc_ref, *, tq, tk, n_kb, lambda_init):
    h = pl.program_id(1)
    qi = pl.program_id(2)
    slope = slopes_ref[h]
    feat_base = (HEAD_DIM, 0)

    @pl.when(qi == 0)
    def _():
        lane = lax.broadcasted_iota(jnp.int32, (tk, HEAD_WIDTH), 1)
        row = lax.broadcasted_iota(jnp.int32, (tk, HEAD_WIDTH), 0)
        for jb in range(n_kb):
            kb = k_ref[0, 0, jb * tk:(jb + 1) * tk, :].astype(_F32)
            for c in range(2):
                feats = _pos_features(
                    jb * tk + row, lane, feat_base[c],
                    lambda hi, lo: (-POS_RADIX * slope, -slope, POS_RADIX * slope * hi, slope * lo))
                own_half = (lane >= c * HEAD_DIM) & (lane < (c + 1) * HEAD_DIM)
                kaug_ref[c, jb] = jnp.where(own_half, kb, feats).astype(_BF16)
            vb = v_ref[0, 0, jb * tk:(jb + 1) * tk, :].astype(_F32)
            vt_ref[jb] = vb.T.astype(_BF16)

    q_t = (q_ref[0, 0].astype(_F32) * (HEAD_DIM ** -0.5)).T
    feat = lax.broadcasted_iota(jnp.int32, (HEAD_WIDTH, tq), 0)
    qpos = qi * tq + lax.broadcasted_iota(jnp.int32, (HEAD_WIDTH, tq), 1)
    q_left, q_right = [], []
    for c in range(2):
        feats = _pos_features(qpos, feat, feat_base[c], lambda hi, lo: (hi, lo, 1.0, 1.0))
        own_half = (feat >= c * HEAD_DIM) & (feat < (c + 1) * HEAD_DIM)
        q_left.append(jnp.where(own_half, q_t, feats).astype(_BF16))
        q_right.append(jnp.where(own_half, q_t, -feats).astype(_BF16))

    def score_block(jb, q_pair, correction):
        block_max = []
        for c in range(2):
            s = jnp.dot(kaug_ref[c, jb], q_pair[c], preferred_element_type=_F32)
            if correction is not None:
                s = s - correction
            s = s * LOG2E
            s_ref[c, jb] = s
            block_max.append(jnp.max(s.reshape(tk // 8, 8, tq), axis=0))
        return tuple(block_max)

    jd = (qi * tq) // tk
    rel = (lax.broadcasted_iota(jnp.int32, (tk, tq), 0)
           - lax.broadcasted_iota(jnp.int32, (tk, tq), 1) + (jd * tk - qi * tq))
    correction = (2.0 * slope) * jnp.maximum(rel, 0).astype(_F32)
    diag_max = score_block(jd, q_left, correction)

    def score_body(d, carry):
        jb = lax.rem(jd + d, n_kb)
        is_left = jb < jd
        q_pair = [jnp.where(is_left, q_left[c], q_right[c]) for c in range(2)]
        block_max = score_block(jb, q_pair, None)
        return tuple(jnp.maximum(a, b) for a, b in zip(carry, block_max))

    col_max = lax.fori_loop(1, n_kb, score_body, diag_max, unroll=True)
    m = [jnp.max(cm, axis=0, keepdims=True) for cm in col_max]

    col_sum = [jnp.zeros((8, tq), _F32), jnp.zeros((8, tq), _F32)]
    for jb in range(n_kb):
        for c in range(2):
            p = jnp.exp2(s_ref[c, jb] - m[c])
            col_sum[c] = col_sum[c] + jnp.sum(p.reshape(tk // 8, 8, tq), axis=0)
            pv = jnp.dot(vt_ref[jb], p.astype(_BF16), preferred_element_type=_F32)
            if jb == 0:
                acc_ref[c] = pv
            else:
                acc_ref[c] += pv
    l = [jnp.sum(cs, axis=0, keepdims=True) for cs in col_sum]
    acc = [acc_ref[0], acc_ref[1]]

    lam_p = lam_ref[...]
    lam = (jnp.exp(jnp.sum(lam_p[0:1] * lam_p[1:2], axis=-1, keepdims=True))
           - jnp.exp(jnp.sum(lam_p[2:3] * lam_p[3:4], axis=-1, keepdims=True))
           + lambda_init)
    o_t = acc[0] / l[0] - lam * (acc[1] / l[1])
    ms = jnp.mean(o_t * o_t, axis=0, keepdims=True)
    o_t = o_t * lax.rsqrt(ms + SUBLN_EPS) * gain_ref[...] * (1.0 - lambda_init)
    o_ref[...] = o_t.T.astype(o_ref.dtype)


def _attention(qkv, slopes, lam_params, gain_col, *, batch, seq, tq, tk, lambda_init):
    _, n_heads, T, _ = qkv.shape
    n_q = seq // tq
    n_kb = seq // tk
    assert tk % tq == 0 and seq <= POS_RADIX * 256
    kern = functools.partial(_attn_kernel, tq=tq, tk=tk, n_kb=n_kb, lambda_init=lambda_init)
    return pl.pallas_call(
        kern,
        grid=(batch, n_heads, n_q),
        in_specs=[
            pl.BlockSpec(memory_space=pltpu.SMEM),
            pl.BlockSpec((4, HEAD_DIM), lambda b, h, i: (0, 0)),
            pl.BlockSpec((HEAD_WIDTH, 1), lambda b, h, i: (0, 0)),
            pl.BlockSpec((1, 1, tq, HEAD_WIDTH), lambda b, h, i: (0, h, b * n_q + i, 0)),
            pl.BlockSpec((1, 1, seq, HEAD_WIDTH), lambda b, h, i: (1, h, b, 0)),
            pl.BlockSpec((1, 1, seq, HEAD_WIDTH), lambda b, h, i: (2, h, b, 0)),
        ],
        out_specs=pl.BlockSpec((tq, HEAD_WIDTH), lambda b, h, i: (b * n_q + i, h)),
        out_shape=jax.ShapeDtypeStruct((T, n_heads * HEAD_WIDTH), _BF16),
        scratch_shapes=[
            pltpu.VMEM((2, n_kb, tk, HEAD_WIDTH), _BF16),
            pltpu.VMEM((n_kb, HEAD_WIDTH, tk), _BF16),
            pltpu.VMEM((2, n_kb, tk, tq), _F32),
            pltpu.VMEM((2, HEAD_WIDTH, tq), _F32),
        ],
        compiler_params=_params("arbitrary", "arbitrary", "arbitrary"),
        name="diff_attn",
    )(slopes, lam_params, gain_col, qkv, qkv, qkv)


def _merge_kernel(o_ref, pz_ref, pzp_ref, pzn_ref, gz_ref, x_ref, bg_ref, pw_ref, ps_ref,
                  wa_ref, wp_ref, wo_ref, g_ref, out_ref, ext_ref, yp_ref, *, tm, seq):
    D = x_ref.shape[1]
    gdim = pw_ref.shape[1]
    t0 = (pl.program_id(0) * tm) % seq
    prev = pzp_ref[...].astype(_F32)
    nxt = pzn_ref[...].astype(_F32)
    ext_ref[0:HALO] = jnp.where(t0 == 0, jnp.zeros_like(prev), prev)
    ext_ref[HALO:HALO + tm] = pz_ref[...].astype(_F32)
    ext_ref[HALO + tm:2 * HALO + tm] = jnp.where(t0 + tm == seq, jnp.zeros_like(nxt), nxt)

    tpos = t0 + lax.broadcasted_iota(jnp.int32, (tm, 1), 0)
    for g, w in enumerate(POOL_WINDOWS):
        half = w // 2
        cols = slice(g * gdim, (g + 1) * gdim)
        win = ext_ref[pl.ds(HALO - half, tm), cols]
        for k in range(1, w):
            win = win + ext_ref[pl.ds(HALO - half + k, tm), cols]
        cnt = (jnp.minimum(tpos + half, seq) - jnp.maximum(tpos - half, 0)).astype(_F32)
        mixed = win / cnt - ext_ref[HALO:HALO + tm, cols]
        yg = jnp.dot(mixed.astype(_BF16), pw_ref[g], preferred_element_type=_F32)
        yp_ref[:, cols] = (yg * ps_ref[:, cols]).astype(_BF16)

    y_attn = jnp.dot(o_ref[...], wa_ref[...], preferred_element_type=_F32)
    y_pool = jnp.dot(yp_ref[...], wp_ref[...], preferred_element_type=_F32)
    gates = jax.nn.sigmoid(gz_ref[...].astype(_F32) + bg_ref[...])
    merged = gates[:, :D] * y_attn + gates[:, D:] * y_pool
    mo = jnp.dot(merged.astype(_BF16), wo_ref[...], preferred_element_type=_F32)
    out_ref[...] = x_ref[...] + _rms(mo, g_ref[...], NORM_EPS)


def _const_spec(shape):
    nd = len(shape)
    return pl.BlockSpec(shape, lambda i: (0,) * nd, pipeline_mode=pl.Buffered(1))


def _merge(o, pz, gz, x2, b_gate, pool_w, pool_scale, w_attn, w_pool, w_out, g_post, *, seq, tm):
    T, D = x2.shape
    aw = o.shape[1]
    hb = tm // HALO
    n_hb = T // HALO
    return pl.pallas_call(
        functools.partial(_merge_kernel, tm=tm, seq=seq),
        grid=(T // tm,),
        in_specs=[
            pl.BlockSpec((tm, aw), lambda i: (i, 0)),
            pl.BlockSpec((tm, aw), lambda i: (i, 0)),
            pl.BlockSpec((HALO, aw), lambda i: (jnp.maximum(i * hb - 1, 0), 0)),
            pl.BlockSpec((HALO, aw), lambda i: (jnp.minimum((i + 1) * hb, n_hb - 1), 0)),
            pl.BlockSpec((tm, 2 * D), lambda i: (i, 0)),
            pl.BlockSpec((tm, D), lambda i: (i, 0)),
            _const_spec(b_gate.shape),
            _const_spec(pool_w.shape),
            _const_spec(pool_scale.shape),
            _const_spec(w_attn.shape),
            _const_spec(w_pool.shape),
            _const_spec(w_out.shape),
            _const_spec(g_post.shape),
        ],
        out_specs=pl.BlockSpec((tm, D), lambda i: (i, 0)),
        out_shape=jax.ShapeDtypeStruct((T, D), _F32),
        scratch_shapes=[
            pltpu.VMEM((tm + 2 * HALO, aw), _F32),
            pltpu.VMEM((tm, aw), _BF16),
        ],
        compiler_params=_params("arbitrary"),
        name="merge",
    )(o, pz, pz, pz, gz, x2, b_gate, pool_w, pool_scale, w_attn, w_pool, w_out, g_post)


def _ffn_kernel(x_ref, xp_ref, xn_ref, g_ref, wg_ref, wv_ref, cwg_ref, cwv_ref, cbg_ref, cbv_ref,
                wd_ref, gp_ref, out_ref, h_ref, acc_ref, *, tm, seq):
    c = pl.program_id(1)
    rows = tm + 2 * F32_SUBLANES

    @pl.when(c == 0)
    def _():
        t0 = (pl.program_id(0) * tm) % seq
        g = g_ref[...]
        hp = _rms(xp_ref[...], g, NORM_EPS)
        hn = _rms(xn_ref[...], g, NORM_EPS)
        hp = jnp.where(t0 == 0, jnp.zeros_like(hp), hp)
        hn = jnp.where(t0 + tm == seq, jnp.zeros_like(hn), hn)
        h_ref[0:tm] = _rms(x_ref[...], g, NORM_EPS).astype(_BF16)
        h_ref[tm:rows] = jnp.concatenate([hn, hp], axis=0).astype(_BF16)
        acc_ref[...] = jnp.zeros_like(acc_ref)

    h = h_ref[...]

    def conv(w_ref, cw_ref, cb_ref):
        u = jnp.dot(h, w_ref[...], preferred_element_type=_F32)
        cw = cw_ref[...]
        u_prev = pltpu.roll(u, 1, axis=0)[0:tm]
        u_next = pltpu.roll(u, rows - 1, axis=0)[0:tm]
        return u_prev * cw[0:1] + u[0:tm] * cw[1:2] + u_next * cw[2:3] + cb_ref[...]

    gate = conv(wg_ref, cwg_ref, cbg_ref)
    val = conv(wv_ref, cwv_ref, cbv_ref)
    act = (jax.nn.gelu(gate) * val).astype(_BF16)
    acc_ref[...] += jnp.dot(act, wd_ref[...], preferred_element_type=_F32)

    @pl.when(c == pl.num_programs(1) - 1)
    def _():
        out_ref[...] = x_ref[...] + _rms(acc_ref[...], gp_ref[...], NORM_EPS)


def _ffn(x1, g_pre, w_up, conv_w, conv_b, w_down, g_post, *, seq, tm, tf):
    T, D = x1.shape
    d_ff = w_down.shape[0]
    n_c = d_ff // tf
    hb = tm // F32_SUBLANES
    n_hb = T // F32_SUBLANES
    return pl.pallas_call(
        functools.partial(_ffn_kernel, tm=tm, seq=seq),
        grid=(T // tm, n_c),
        in_specs=[
            pl.BlockSpec((tm, D), lambda i, c: (i, 0)),
            pl.BlockSpec((F32_SUBLANES, D), lambda i, c: (jnp.maximum(i * hb - 1, 0), 0)),
            pl.BlockSpec((F32_SUBLANES, D), lambda i, c: (jnp.minimum((i + 1) * hb, n_hb - 1), 0)),
            pl.BlockSpec((1, D), lambda i, c: (0, 0)),
            pl.BlockSpec((D, tf), lambda i, c: (0, c)),
            pl.BlockSpec((D, tf), lambda i, c: (0, n_c + c)),
            pl.BlockSpec((CONV_WIDTH, tf), lambda i, c: (0, c)),
            pl.BlockSpec((CONV_WIDTH, tf), lambda i, c: (0, n_c + c)),
            pl.BlockSpec((1, tf), lambda i, c: (0, c)),
            pl.BlockSpec((1, tf), lambda i, c: (0, n_c + c)),
            pl.BlockSpec((tf, D), lambda i, c: (c, 0)),
            pl.BlockSpec((1, D), lambda i, c: (0, 0)),
        ],
        out_specs=pl.BlockSpec((tm, D), lambda i, c: (i, 0)),
        out_shape=jax.ShapeDtypeStruct((T, D), _F32),
        scratch_shapes=[
            pltpu.VMEM((tm + 2 * F32_SUBLANES, D), _BF16),
            pltpu.VMEM((tm, D), _F32),
        ],
        compiler_params=_params("arbitrary", "arbitrary"),
        name="conv_ffn",
    )(x1, x1, x1, g_pre, w_up, w_up, conv_w, conv_w, conv_b, conv_b, w_down, g_post)


def _pick(n, pref):
    t = min(n, pref)
    while n % t:
        t //= 2
    return t


def kernel(x, g_pre_mix, w_in, b_gate, lambda_q1, lambda_k1, lambda_q2, lambda_k2, subln_gain,
           w_attn_branch, pool_w, pool_scale, w_pool_branch, w_out, g_post_mix, g_pre_ffn, w_up,
           conv_w, conv_b, w_down, g_post_ffn):
    B, S, D = x.shape
    depth = w_in.shape[0]
    aw = w_attn_branch.shape[1]
    n_heads = aw // HEAD_WIDTH
    d_ff = w_down.shape[1]
    T = B * S
    slopes_np = np.array([2.0 ** (-8.0 * (h + 1) / n_heads) for h in range(n_heads)], dtype=np.float32)
    assert np.all(np.frexp(slopes_np)[0] == 0.5)
    slopes = jnp.asarray(slopes_np)
    row = lambda a: a.reshape(1, -1).astype(_F32)

    x2 = x.reshape(T, D)
    for l in range(depth):
        lambda_init = 0.8 - 0.6 * math.exp(-0.3 * l)
        qkv, pz, gz = _in_proj(x2, row(g_pre_mix[l]), w_in[l].astype(_BF16),
                               n_heads=n_heads, tm=_pick(T, 1024))
        lam_params = jnp.stack([lambda_q1[l], lambda_k1[l], lambda_q2[l], lambda_k2[l]]).astype(_F32)
        o = _attention(qkv, slopes, lam_params, subln_gain[l].reshape(-1, 1).astype(_F32),
                       batch=B, seq=S, tq=_pick(S, 256), tk=_pick(S, 512), lambda_init=lambda_init)
        x2 = _merge(o, pz, gz, x2, row(b_gate[l]), pool_w[l].astype(_BF16), row(pool_scale[l]),
                    w_attn_branch[l].astype(_BF16), w_pool_branch[l].astype(_BF16),
                    w_out[l].astype(_BF16), row(g_post_mix[l]), seq=S, tm=_pick(S, 256))
        x2 = _ffn(x2, row(g_pre_ffn[l]), w_up[l].astype(_BF16), conv_w[l].astype(_F32),
                  row(conv_b[l]), w_down[l].astype(_BF16), row(g_post_ffn[l]),
                  seq=S, tm=_pick(S, 512), tf=_pick(d_ff, 512))
    return x2.reshape(B, S, D)
```

```python
import functools
import math

import jax
import jax.numpy as jnp
import numpy as np
from jax import lax
from jax.experimental import pallas as pl
from jax.experimental.pallas import tpu as pltpu

HEAD_DIM = 64
HEAD_WIDTH = 2 * HEAD_DIM
POOL_WINDOWS = (2, 4, 8, 16)
CONV_WIDTH = 3
NORM_EPS = 1e-6
SUBLN_EPS = 1e-5
HALO = 16
F32_SUBLANES = 8
POS_RADIX = 64
LOG2E = math.log2(math.e)
V7X_VMEM_LIMIT_BYTES = 56 * 1024 * 1024

_F32 = jnp.float32
_BF16 = jnp.bfloat16


def _rms(x, g, eps):
    return x * lax.rsqrt(jnp.mean(x * x, axis=-1, keepdims=True) + eps) * g


def _params(*sem):
    return pltpu.CompilerParams(dimension_semantics=sem, vmem_limit_bytes=V7X_VMEM_LIMIT_BYTES)


def _in_proj_kernel(x_ref, g_ref, w_ref, qkv_ref, pz_ref, gz_ref, h_ref, *, n_heads):
    j = pl.program_id(1)

    @pl.when(j == 0)
    def _():
        h_ref[...] = _rms(x_ref[...], g_ref[...], NORM_EPS).astype(_BF16)

    z = jnp.dot(h_ref[...], w_ref[...], preferred_element_type=_F32).astype(_BF16)

    @pl.when(j < 3)
    def _():
        for hh in range(n_heads):
            qkv_ref[0, hh] = z[:, hh * HEAD_WIDTH:(hh + 1) * HEAD_WIDTH]

    @pl.when(j == 3)
    def _():
        pz_ref[...] = z

    @pl.when(j > 3)
    def _():
        gz_ref[...] = z


def _in_proj(x2, g, w, *, n_heads, tm):
    T, D = x2.shape
    aw = n_heads * HEAD_WIDTH
    n_gate_blocks = 2 * D // aw
    grid = (T // tm, 4 + n_gate_blocks)
    return pl.pallas_call(
        functools.partial(_in_proj_kernel, n_heads=n_heads),
        grid=grid,
        in_specs=[
            pl.BlockSpec((tm, D), lambda i, j: (i, 0)),
            pl.BlockSpec((1, D), lambda i, j: (0, 0)),
            pl.BlockSpec((D, aw), lambda i, j: (0, j)),
        ],
        out_specs=[
            pl.BlockSpec((1, n_heads, tm, HEAD_WIDTH), lambda i, j: (jnp.minimum(j, 2), 0, i, 0)),
            pl.BlockSpec((tm, aw), lambda i, j: (i, 0)),
            pl.BlockSpec((tm, aw), lambda i, j: (i, jnp.maximum(j - 4, 0))),
        ],
        out_shape=[
            jax.ShapeDtypeStruct((3, n_heads, T, HEAD_WIDTH), _BF16),
            jax.ShapeDtypeStruct((T, aw), _BF16),
            jax.ShapeDtypeStruct((T, 2 * D), _BF16),
        ],
        scratch_shapes=[pltpu.VMEM((tm, D), _BF16)],
        compiler_params=_params("arbitrary", "arbitrary"),
        name="in_proj",
    )(x2, g, w)


def _pos_features(pos, idx, base, values):
    hi = (pos // POS_RADIX).astype(_F32)
    lo = (pos % POS_RADIX).astype(_F32)
    out = jnp.zeros(pos.shape, _F32)
    for n, val in enumerate(values(hi, lo)):
        out = jnp.where(idx == base + n, val, out)
    return out


def _attn_kernel(slopes_ref, lam_ref, gain_ref, q_ref, k_ref, v_ref, o_ref,
                 kaug_ref, vt_ref, corr_ref, s_even_ref, s_odd_ref, m_even_ref, m_odd_ref, acc_ref,
                 *, tq, tk, n_kb, n_q, lambda_init):
    h = pl.program_id(1)
    step = pl.program_id(2)
    qi = jnp.minimum(step, n_q - 1)
    slope = slopes_ref[h]
    feat_base = (HEAD_DIM, 0)

    @pl.when((pl.program_id(0) == 0) & (h == 0) & (step == 0))
    def _():
        s_odd_ref[...] = jnp.zeros_like(s_odd_ref)
        m_odd_ref[...] = jnp.zeros_like(m_odd_ref)

    @pl.when(step == 0)
    def _():
        lane = lax.broadcasted_iota(jnp.int32, (tk, HEAD_WIDTH), 1)
        row = lax.broadcasted_iota(jnp.int32, (tk, HEAD_WIDTH), 0)
        feats, feats_per_block, own_half = [], [], []
        for c in range(2):
            feats.append(_pos_features(
                row, lane, feat_base[c],
                lambda hi, lo: (-POS_RADIX * slope, -slope, POS_RADIX * slope * hi, slope * lo)))
            feats_per_block.append(jnp.where(lane == feat_base[c] + 2, slope * tk, 0.0))
            own_half.append((lane >= c * HEAD_DIM) & (lane < (c + 1) * HEAD_DIM))
        for jb in range(n_kb):
            kb = k_ref[0, 0, jb * tk:(jb + 1) * tk, :].astype(_F32)
            for c in range(2):
                kaug_ref[c, jb] = jnp.where(own_half[c], kb, feats[c]).astype(_BF16)
                feats[c] = feats[c] + feats_per_block[c]
            vb = v_ref[0, 0, jb * tk:(jb + 1) * tk, :].astype(_F32)
            vt_ref[jb] = vb.T.astype(_BF16)
        rel = (lax.broadcasted_iota(jnp.int32, (tk, tq), 0)
               - lax.broadcasted_iota(jnp.int32, (tk, tq), 1))
        for v in range(tk // tq):
            corr_ref[v] = (2.0 * slope) * jnp.maximum(rel - v * tq, 0).astype(_F32)

    q_t = (q_ref[0, 0].astype(_F32) * (HEAD_DIM ** -0.5)).T
    q_half = (q_t[:HEAD_DIM], q_t[HEAD_DIM:])
    sub = lax.broadcasted_iota(jnp.int32, (F32_SUBLANES, tq), 0)
    qfeat = _pos_features(qi * tq + lax.broadcasted_iota(jnp.int32, (F32_SUBLANES, tq), 1),
                          sub, 0, lambda hi, lo: (hi, lo, 1.0, 1.0))
    no_feat = jnp.zeros((HEAD_DIM - F32_SUBLANES, tq), _F32)

    def with_features(c, f):
        other_half = jnp.concatenate([f, no_feat], axis=0)
        parts = (q_half[0], other_half) if c == 0 else (other_half, q_half[1])
        return jnp.concatenate(parts, axis=0).astype(_BF16)

    q_left = [with_features(c, qfeat) for c in range(2)]
    q_right = [with_features(c, -qfeat) for c in range(2)]

    jd = (qi * tq) // tk
    correction = corr_ref[(qi * tq - jd * tk) // tq]

    lam_p = lam_ref[...]
    lam = (jnp.exp(jnp.sum(lam_p[0:1] * lam_p[1:2], axis=-1, keepdims=True))
           - jnp.exp(jnp.sum(lam_p[2:3] * lam_p[3:4], axis=-1, keepdims=True))
           + lambda_init)

    def run_step(s_new_ref, m_new_ref, s_old_ref, m_old_ref):
        def score_block(d, col_max):
            jb = jd if d == 0 else lax.rem(jd + d, n_kb)
            out = []
            for c in range(2):
                q_c = q_left[c] if d == 0 else jnp.where(jb < jd, q_left[c], q_right[c])
                s = jnp.dot(kaug_ref[c, jb], q_c, preferred_element_type=_F32)
                if d == 0:
                    s = s - correction
                s = s * LOG2E
                s_new_ref[c, jb] = s
                block_max = jnp.max(s.reshape(tk // 8, 8, tq), axis=0)
                out.append(block_max if col_max is None else jnp.maximum(col_max[c], block_max))
            return out

        m_old = [jnp.max(m_old_ref[c], axis=0, keepdims=True) for c in range(2)]
        col_sum = [jnp.zeros((8, tq), _F32), jnp.zeros((8, tq), _F32)]
        col_max = None
        for jb in range(n_kb):
            col_max = score_block(jb, col_max)
            for c in range(2):
                p = jnp.exp2(s_old_ref[c, jb] - m_old[c])
                col_sum[c] = col_sum[c] + jnp.sum(p.reshape(tk // 8, 8, tq), axis=0)
                pv = jnp.dot(vt_ref[jb], p.astype(_BF16), preferred_element_type=_F32)
                if jb == 0:
                    acc_ref[c] = pv
                else:
                    acc_ref[c] += pv
        for c in range(2):
            m_new_ref[c] = col_max[c]

        l = [jnp.sum(cs, axis=0, keepdims=True) for cs in col_sum]
        o_t = acc_ref[0] / l[0] - lam * (acc_ref[1] / l[1])
        ms = jnp.mean(o_t * o_t, axis=0, keepdims=True)
        o_t = o_t * lax.rsqrt(ms + SUBLN_EPS) * gain_ref[...] * (1.0 - lambda_init)
        o_ref[...] = o_t.T.astype(o_ref.dtype)

    @pl.when(step % 2 == 0)
    def _():
        run_step(s_even_ref, m_even_ref, s_odd_ref, m_odd_ref)

    @pl.when(step % 2 == 1)
    def _():
        run_step(s_odd_ref, m_odd_ref, s_even_ref, m_even_ref)


def _attention(qkv, slopes, lam_params, gain_col, *, batch, seq, tq, tk, lambda_init):
    _, n_heads, T, _ = qkv.shape
    n_q = seq // tq
    n_kb = seq // tk
    assert tk % tq == 0 and seq <= POS_RADIX * 256
    kern = functools.partial(_attn_kernel, tq=tq, tk=tk, n_kb=n_kb, n_q=n_q, lambda_init=lambda_init)
    scores = pltpu.VMEM((2, n_kb, tk, tq), _F32)
    col_max = pltpu.VMEM((2, F32_SUBLANES, tq), _F32)
    return pl.pallas_call(
        kern,
        grid=(batch, n_heads, n_q + 1),
        in_specs=[
            pl.BlockSpec(memory_space=pltpu.SMEM),
            pl.BlockSpec((4, HEAD_DIM), lambda b, h, i: (0, 0)),
            pl.BlockSpec((HEAD_WIDTH, 1), lambda b, h, i: (0, 0)),
            pl.BlockSpec((1, 1, tq, HEAD_WIDTH),
                         lambda b, h, i: (0, h, b * n_q + jnp.minimum(i, n_q - 1), 0)),
            pl.BlockSpec((1, 1, seq, HEAD_WIDTH), lambda b, h, i: (1, h, b, 0)),
            pl.BlockSpec((1, 1, seq, HEAD_WIDTH), lambda b, h, i: (2, h, b, 0)),
        ],
        out_specs=pl.BlockSpec((tq, HEAD_WIDTH), lambda b, h, i: (b * n_q + jnp.maximum(i - 1, 0), h)),
        out_shape=jax.ShapeDtypeStruct((T, n_heads * HEAD_WIDTH), _BF16),
        scratch_shapes=[
            pltpu.VMEM((2, n_kb, tk, HEAD_WIDTH), _BF16),
            pltpu.VMEM((n_kb, HEAD_WIDTH, tk), _BF16),
            pltpu.VMEM((tk // tq, tk, tq), _F32),
            scores, scores, col_max, col_max,
            pltpu.VMEM((2, HEAD_WIDTH, tq), _F32),
        ],
        compiler_params=_params("arbitrary", "arbitrary", "arbitrary"),
        name="diff_attn",
    )(slopes, lam_params, gain_col, qkv, qkv, qkv)


def _merge_kernel(o_ref, pz_ref, pzp_ref, pzn_ref, gz_ref, x_ref, bg_ref, pw_ref, ps_ref,
                  wa_ref, wp_ref, wo_ref, g_ref, out_ref, ext_ref, yp_ref, *, tm, seq):
    D = x_ref.shape[1]
    gdim = pw_ref.shape[1]
    t0 = (pl.program_id(0) * tm) % seq
    prev = pzp_ref[...].astype(_F32)
    nxt = pzn_ref[...].astype(_F32)
    ext_ref[0:HALO] = jnp.where(t0 == 0, jnp.zeros_like(prev), prev)
    ext_ref[HALO:HALO + tm] = pz_ref[...].astype(_F32)
    ext_ref[HALO + tm:2 * HALO + tm] = jnp.where(t0 + tm == seq, jnp.zeros_like(nxt), nxt)

    tpos = t0 + lax.broadcasted_iota(jnp.int32, (tm, 1), 0)
    for g, w in enumerate(POOL_WINDOWS):
        half = w // 2
        cols = slice(g * gdim, (g + 1) * gdim)
        win = ext_ref[pl.ds(HALO - half, tm), cols]
        for k in range(1, w):
            win = win + ext_ref[pl.ds(HALO - half + k, tm), cols]
        cnt = (jnp.minimum(tpos + half, seq) - jnp.maximum(tpos - half, 0)).astype(_F32)
        mixed = win / cnt - ext_ref[HALO:HALO + tm, cols]
        yg = jnp.dot(mixed.astype(_BF16), pw_ref[g], preferred_element_type=_F32)
        yp_ref[:, cols] = (yg * ps_ref[:, cols]).astype(_BF16)

    y_attn = jnp.dot(o_ref[...], wa_ref[...], preferred_element_type=_F32)
    y_pool = jnp.dot(yp_ref[...], wp_ref[...], preferred_element_type=_F32)
    gates = jax.nn.sigmoid(gz_ref[...].astype(_F32) + bg_ref[...])
    merged = gates[:, :D] * y_attn + gates[:, D:] * y_pool
    mo = jnp.dot(merged.astype(_BF16), wo_ref[...], preferred_element_type=_F32)
    out_ref[...] = x_ref[...] + _rms(mo, g_ref[...], NORM_EPS)


def _const_spec(shape):
    nd = len(shape)
    return pl.BlockSpec(shape, lambda i: (0,) * nd, pipeline_mode=pl.Buffered(1))


def _merge(o, pz, gz, x2, b_gate, pool_w, pool_scale, w_attn, w_pool, w_out, g_post, *, seq, tm):
    T, D = x2.shape
    aw = o.shape[1]
    hb = tm // HALO
    n_hb = T // HALO
    return pl.pallas_call(
        functools.partial(_merge_kernel, tm=tm, seq=seq),
        grid=(T // tm,),
        in_specs=[
            pl.BlockSpec((tm, aw), lambda i: (i, 0)),
            pl.BlockSpec((tm, aw), lambda i: (i, 0)),
            pl.BlockSpec((HALO, aw), lambda i: (jnp.maximum(i * hb - 1, 0), 0)),
            pl.BlockSpec((HALO, aw), lambda i: (jnp.minimum((i + 1) * hb, n_hb - 1), 0)),
            pl.BlockSpec((tm, 2 * D), lambda i: (i, 0)),
            pl.BlockSpec((tm, D), lambda i: (i, 0)),
            _const_spec(b_gate.shape),
            _const_spec(pool_w.shape),
            _const_spec(pool_scale.shape),
            _const_spec(w_attn.shape),
            _const_spec(w_pool.shape),
            _const_spec(w_out.shape),
            _const_spec(g_post.shape),
        ],
        out_specs=pl.BlockSpec((tm, D), lambda i: (i, 0)),
        out_shape=jax.ShapeDtypeStruct((T, D), _F32),
        scratch_shapes=[
            pltpu.VMEM((tm + 2 * HALO, aw), _F32),
            pltpu.VMEM((tm, aw), _BF16),
        ],
        compiler_params=_params("arbitrary"),
        name="merge",
    )(o, pz, pz, pz, gz, x2, b_gate, pool_w, pool_scale, w_attn, w_pool, w_out, g_post)


def _ffn_kernel(x_ref, xp_ref, xn_ref, g_ref, wg_ref, wv_ref, cwg_ref, cwv_ref, cbg_ref, cbv_ref,
                wd_ref, gp_ref, out_ref, h_ref, acc_ref, *, tm, seq):
    c = pl.program_id(1)
    rows = tm + 2 * HALO

    @pl.when(c == 0)
    def _():
        t0 = (pl.program_id(0) * tm) % seq
        g = g_ref[...]
        hp = _rms(xp_ref[...], g, NORM_EPS)
        hn = _rms(xn_ref[...], g, NORM_EPS)
        h_ref[0:HALO] = jnp.where(t0 == 0, jnp.zeros_like(hp), hp).astype(_BF16)
        h_ref[HALO:HALO + tm] = _rms(x_ref[...], g, NORM_EPS).astype(_BF16)
        h_ref[HALO + tm:rows] = jnp.where(t0 + tm == seq, jnp.zeros_like(hn), hn).astype(_BF16)
        acc_ref[...] = jnp.zeros_like(acc_ref)

    h = h_ref[...]

    def conv(w_ref, cw_ref, cb_ref):
        u = jnp.dot(h, w_ref[...], preferred_element_type=_F32)
        cw = cw_ref[...]
        u_prev = pltpu.roll(u, 1, axis=0)[HALO:HALO + tm]
        u_next = pltpu.roll(u, rows - 1, axis=0)[HALO:HALO + tm]
        return u_prev * cw[0:1] + u[HALO:HALO + tm] * cw[1:2] + u_next * cw[2:3] + cb_ref[...]

    gate = conv(wg_ref, cwg_ref, cbg_ref)
    val = conv(wv_ref, cwv_ref, cbv_ref)
    act = (jax.nn.gelu(gate) * val).astype(_BF16)
    acc_ref[...] += jnp.dot(act, wd_ref[...], preferred_element_type=_F32)

    @pl.when(c == pl.num_programs(1) - 1)
    def _():
        out_ref[...] = x_ref[...] + _rms(acc_ref[...], gp_ref[...], NORM_EPS)


def _ffn(x1, g_pre, w_up, conv_w, conv_b, w_down, g_post, *, seq, tm, tf):
    T, D = x1.shape
    d_ff = w_down.shape[0]
    n_c = d_ff // tf
    hb = tm // HALO
    n_hb = T // HALO
    return pl.pallas_call(
        functools.partial(_ffn_kernel, tm=tm, seq=seq),
        grid=(T // tm, n_c),
        in_specs=[
            pl.BlockSpec((tm, D), lambda i, c: (i, 0)),
            pl.BlockSpec((HALO, D), lambda i, c: (jnp.maximum(i * hb - 1, 0), 0)),
            pl.BlockSpec((HALO, D), lambda i, c: (jnp.minimum((i + 1) * hb, n_hb - 1), 0)),
            pl.BlockSpec((1, D), lambda i, c: (0, 0)),
            pl.BlockSpec((D, tf), lambda i, c: (0, c)),
            pl.BlockSpec((D, tf), lambda i, c: (0, n_c + c)),
            pl.BlockSpec((CONV_WIDTH, tf), lambda i, c: (0, c)),
            pl.BlockSpec((CONV_WIDTH, tf), lambda i, c: (0, n_c + c)),
            pl.BlockSpec((1, tf), lambda i, c: (0, c)),
            pl.BlockSpec((1, tf), lambda i, c: (0, n_c + c)),
            pl.BlockSpec((tf, D), lambda i, c: (c, 0)),
            pl.BlockSpec((1, D), lambda i, c: (0, 0)),
        ],
        out_specs=pl.BlockSpec((tm, D), lambda i, c: (i, 0)),
        out_shape=jax.ShapeDtypeStruct((T, D), _F32),
        scratch_shapes=[
            pltpu.VMEM((tm + 2 * HALO, D), _BF16),
            pltpu.VMEM((tm, D), _F32),
        ],
        compiler_params=_params("arbitrary", "arbitrary"),
        name="conv_ffn",
    )(x1, x1, x1, g_pre, w_up, w_up, conv_w, conv_w, conv_b, conv_b, w_down, g_post)


def _pick(n, pref):
    t = min(n, pref)
    while n % t:
        t //= 2
    return t


def kernel(x, g_pre_mix, w_in, b_gate, lambda_q1, lambda_k1, lambda_q2, lambda_k2, subln_gain,
           w_attn_branch, pool_w, pool_scale, w_pool_branch, w_out, g_post_mix, g_pre_ffn, w_up,
           conv_w, conv_b, w_down, g_post_ffn):
    B, S, D = x.shape
    depth = w_in.shape[0]
    aw = w_attn_branch.shape[1]
    n_heads = aw // HEAD_WIDTH
    d_ff = w_down.shape[1]
    T = B * S
    slopes_np = np.array([2.0 ** (-8.0 * (h + 1) / n_heads) for h in range(n_heads)], dtype=np.float32)
    assert np.all(np.frexp(slopes_np)[0] == 0.5)
    slopes = jnp.asarray(slopes_np)
    row = lambda a: a.reshape(1, -1).astype(_F32)

    x2 = x.reshape(T, D)
    for l in range(depth):
        lambda_init = 0.8 - 0.6 * math.exp(-0.3 * l)
        qkv, pz, gz = _in_proj(x2, row(g_pre_mix[l]), w_in[l].astype(_BF16),
                               n_heads=n_heads, tm=_pick(T, 1024))
        lam_params = jnp.stack([lambda_q1[l], lambda_k1[l], lambda_q2[l], lambda_k2[l]]).astype(_F32)
        o = _attention(qkv, slopes, lam_params, subln_gain[l].reshape(-1, 1).astype(_F32),
                       batch=B, seq=S, tq=_pick(S, 256), tk=_pick(S, 512), lambda_init=lambda_init)
        x2 = _merge(o, pz, gz, x2, row(b_gate[l]), pool_w[l].astype(_BF16), row(pool_scale[l]),
                    w_attn_branch[l].astype(_BF16), w_pool_branch[l].astype(_BF16),
                    w_out[l].astype(_BF16), row(g_post_mix[l]), seq=S, tm=_pick(S, 256))
        x2 = _ffn(x2, row(g_pre_ffn[l]), w_up[l].astype(_BF16), conv_w[l].astype(_F32),
                  row(conv_b[l]), w_down[l].astype(_BF16), row(g_post_ffn[l]),
                  seq=S, tm=_pick(S, 512), tf=_pick(d_ff, 512))
    return x2.reshape(B, S, D)
```

```python
import functools
import math

import jax
import jax.numpy as jnp
import numpy as np
from jax import lax
from jax.experimental import pallas as pl
from jax.experimental.pallas import tpu as pltpu

HEAD_DIM = 64
HEAD_WIDTH = 2 * HEAD_DIM
POOL_WINDOWS = (2, 4, 8, 16)
CONV_WIDTH = 3
NORM_EPS = 1e-6
SUBLN_EPS = 1e-5
HALO = 16
F32_SUBLANES = 8
POS_RADIX = 64
LOG2E = math.log2(math.e)
V7X_VMEM_LIMIT_BYTES = 56 * 1024 * 1024

_F32 = jnp.float32
_BF16 = jnp.bfloat16


def _rms(x, g, eps):
    return x * lax.rsqrt(jnp.mean(x * x, axis=-1, keepdims=True) + eps) * g


def _params(*sem):
    return pltpu.CompilerParams(dimension_semantics=sem, vmem_limit_bytes=V7X_VMEM_LIMIT_BYTES)


def _in_proj_kernel(x_ref, g_ref, w_ref, qkv_ref, pz_ref, gz_ref, h_ref, *, n_heads):
    j = pl.program_id(1)

    @pl.when(j == 0)
    def _():
        h_ref[...] = _rms(x_ref[...], g_ref[...], NORM_EPS).astype(_BF16)

    z = jnp.dot(h_ref[...], w_ref[...], preferred_element_type=_F32).astype(_BF16)

    @pl.when(j < 3)
    def _():
        for hh in range(n_heads):
            qkv_ref[0, hh] = z[:, hh * HEAD_WIDTH:(hh + 1) * HEAD_WIDTH]

    @pl.when(j == 3)
    def _():
        pz_ref[...] = z

    @pl.when(j > 3)
    def _():
        gz_ref[...] = z


def _in_proj(x2, g, w, *, n_heads, tm):
    T, D = x2.shape
    aw = n_heads * HEAD_WIDTH
    n_gate_blocks = 2 * D // aw
    grid = (T // tm, 4 + n_gate_blocks)
    return pl.pallas_call(
        functools.partial(_in_proj_kernel, n_heads=n_heads),
        grid=grid,
        in_specs=[
            pl.BlockSpec((tm, D), lambda i, j: (i, 0)),
            pl.BlockSpec((1, D), lambda i, j: (0, 0)),
            pl.BlockSpec((D, aw), lambda i, j: (0, j)),
        ],
        out_specs=[
            pl.BlockSpec((1, n_heads, tm, HEAD_WIDTH), lambda i, j: (jnp.minimum(j, 2), 0, i, 0)),
            pl.BlockSpec((tm, aw), lambda i, j: (i, 0)),
            pl.BlockSpec((tm, aw), lambda i, j: (i, jnp.maximum(j - 4, 0))),
        ],
        out_shape=[
            jax.ShapeDtypeStruct((3, n_heads, T, HEAD_WIDTH), _BF16),
            jax.ShapeDtypeStruct((T, aw), _BF16),
            jax.ShapeDtypeStruct((T, 2 * D), _BF16),
        ],
        scratch_shapes=[pltpu.VMEM((tm, D), _BF16)],
        compiler_params=_params("arbitrary", "arbitrary"),
        name="in_proj",
    )(x2, g, w)


def _pos_features(pos, idx, base, values):
    hi = (pos // POS_RADIX).astype(_F32)
    lo = (pos % POS_RADIX).astype(_F32)
    out = jnp.zeros(pos.shape, _F32)
    for n, val in enumerate(values(hi, lo)):
        out = jnp.where(idx == base + n, val, out)
    return out


def _attn_kernel(slopes_ref, lam_ref, gain_ref, q_ref, k_ref, v_ref, o_ref,
                 kaug_ref, vt_ref, corr_ref, s_even_ref, s_odd_ref, m_even_ref, m_odd_ref,
                 acc_even_ref, acc_odd_ref, l_even_ref, l_odd_ref,
                 *, tq, tk, n_kb, n_q, n_heads, n_tiles, lambda_init):
    step = pl.program_id(0)
    tile = jnp.minimum(step, n_tiles - 1)
    head = tile // n_q
    qi = tile % n_q
    prev_head = jnp.minimum(jnp.maximum(step - 1, 0), n_tiles - 1) // n_q
    feat_base = (HEAD_DIM, 0)

    @pl.when(step == 0)
    def _():
        s_odd_ref[...] = jnp.zeros_like(s_odd_ref)
        m_odd_ref[...] = jnp.zeros_like(m_odd_ref)
        acc_odd_ref[...] = jnp.zeros_like(acc_odd_ref)
        l_odd_ref[...] = jnp.ones_like(l_odd_ref)

    @pl.when((qi == 0) & (step < n_tiles))
    def _():
        slope = slopes_ref[head % n_heads]
        lane = lax.broadcasted_iota(jnp.int32, (tk, HEAD_WIDTH), 1)
        row = lax.broadcasted_iota(jnp.int32, (tk, HEAD_WIDTH), 0)
        feats, feats_per_block, own_half = [], [], []
        for c in range(2):
            feats.append(_pos_features(
                row, lane, feat_base[c],
                lambda hi, lo: (-POS_RADIX * slope, -slope, POS_RADIX * slope * hi, slope * lo)))
            feats_per_block.append(jnp.where(lane == feat_base[c] + 2, slope * tk, 0.0))
            own_half.append((lane >= c * HEAD_DIM) & (lane < (c + 1) * HEAD_DIM))
        for jb in range(n_kb):
            kb = k_ref[0, 0, jb * tk:(jb + 1) * tk, :].astype(_F32)
            for c in range(2):
                kaug_ref[c, jb] = jnp.where(own_half[c], kb, feats[c]).astype(_BF16)
                feats[c] = feats[c] + feats_per_block[c]
            vb = v_ref[0, 0, jb * tk:(jb + 1) * tk, :].astype(_F32)
            vt_ref[head % 2, jb] = vb.T.astype(_BF16)
        rel = (lax.broadcasted_iota(jnp.int32, (tk, tq), 0)
               - lax.broadcasted_iota(jnp.int32, (tk, tq), 1))
        for v in range(tk // tq):
            corr_ref[v] = (2.0 * slope) * jnp.maximum(rel - v * tq, 0).astype(_F32)

    q_t = (q_ref[0, 0].astype(_F32) * (HEAD_DIM ** -0.5)).T
    q_half = (q_t[:HEAD_DIM], q_t[HEAD_DIM:])
    sub = lax.broadcasted_iota(jnp.int32, (F32_SUBLANES, tq), 0)
    qfeat = _pos_features(qi * tq + lax.broadcasted_iota(jnp.int32, (F32_SUBLANES, tq), 1),
                          sub, 0, lambda hi, lo: (hi, lo, 1.0, 1.0))
    no_feat = jnp.zeros((HEAD_DIM - F32_SUBLANES, tq), _F32)

    def with_features(c, f):
        other_half = jnp.concatenate([f, no_feat], axis=0)
        parts = (q_half[0], other_half) if c == 0 else (other_half, q_half[1])
        return jnp.concatenate(parts, axis=0).astype(_BF16)

    q_left = [with_features(c, qfeat) for c in range(2)]
    q_right = [with_features(c, -qfeat) for c in range(2)]

    jd = (qi * tq) // tk
    correction = corr_ref[(qi * tq - jd * tk) // tq]

    lam_p = lam_ref[...]
    lam = (jnp.exp(jnp.sum(lam_p[0:1] * lam_p[1:2], axis=-1, keepdims=True))
           - jnp.exp(jnp.sum(lam_p[2:3] * lam_p[3:4], axis=-1, keepdims=True))
           + lambda_init)

    def run_step(s_new_ref, m_new_ref, acc_new_ref, l_new_ref,
                 s_old_ref, m_old_ref, acc_old_ref, l_old_ref):
        l_old = [jnp.sum(l_old_ref[c], axis=0, keepdims=True) for c in range(2)]
        o_t = acc_old_ref[0] / l_old[0] - lam * (acc_old_ref[1] / l_old[1])
        ms = jnp.mean(o_t * o_t, axis=0, keepdims=True)
        o_t = o_t * lax.rsqrt(ms + SUBLN_EPS) * gain_ref[...] * (1.0 - lambda_init)
        o_ref[...] = o_t.T.astype(o_ref.dtype)

        def score_block(d, col_max):
            jb = jd if d == 0 else lax.rem(jd + d, n_kb)
            out = []
            for c in range(2):
                q_c = q_left[c] if d == 0 else jnp.where(jb < jd, q_left[c], q_right[c])
                s = jnp.dot(kaug_ref[c, jb], q_c, preferred_element_type=_F32)
                if d == 0:
                    s = s - correction
                s = s * LOG2E
                s_new_ref[c, jb] = s
                block_max = jnp.max(s.reshape(tk // 8, 8, tq), axis=0)
                out.append(block_max if col_max is None else jnp.maximum(col_max[c], block_max))
            return out

        m_old = [jnp.max(m_old_ref[c], axis=0, keepdims=True) for c in range(2)]
        col_sum = [jnp.zeros((8, tq), _F32), jnp.zeros((8, tq), _F32)]
        col_max = None
        for jb in range(n_kb):
            col_max = score_block(jb, col_max)
            v_t = vt_ref[prev_head % 2, jb]
            for c in range(2):
                p = jnp.exp2(s_old_ref[c, jb] - m_old[c])
                col_sum[c] = col_sum[c] + jnp.sum(p.reshape(tk // 8, 8, tq), axis=0)
                pv = jnp.dot(v_t, p.astype(_BF16), preferred_element_type=_F32)
                if jb == 0:
                    acc_new_ref[c] = pv
                else:
                    acc_new_ref[c] += pv
        for c in range(2):
            m_new_ref[c] = col_max[c]
            l_new_ref[c] = col_sum[c]

    even = (s_even_ref, m_even_ref, acc_even_ref, l_even_ref)
    odd = (s_odd_ref, m_odd_ref, acc_odd_ref, l_odd_ref)

    @pl.when(step % 2 == 0)
    def _():
        run_step(*even, *odd)

    @pl.when(step % 2 == 1)
    def _():
        run_step(*odd, *even)


def _attention(qkv, slopes, lam_params, gain_col, *, batch, seq, tq, tk, lambda_init):
    _, n_heads, T, _ = qkv.shape
    n_q = seq // tq
    n_kb = seq // tk
    n_tiles = batch * n_heads * n_q
    assert tk % tq == 0 and seq <= POS_RADIX * 256
    kern = functools.partial(_attn_kernel, tq=tq, tk=tk, n_kb=n_kb, n_q=n_q, n_heads=n_heads,
                             n_tiles=n_tiles, lambda_init=lambda_init)

    def score_tile(s):
        t = jnp.minimum(s, n_tiles - 1)
        return t // (n_heads * n_q), (t // n_q) % n_heads, t % n_q

    def output_tile(s):
        return score_tile(jnp.maximum(s - 2, 0))

    def q_index(s):
        b, h, i = score_tile(s)
        return 0, h, b * n_q + i, 0

    def kv_index(which):
        def index(s):
            b, h, _ = score_tile(s)
            return which, h, b, 0
        return index

    def out_index(s):
        b, h, i = output_tile(s)
        return b * n_q + i, h

    scores = pltpu.VMEM((2, n_kb, tk, tq), _F32)
    col_stat = pltpu.VMEM((2, F32_SUBLANES, tq), _F32)
    acc = pltpu.VMEM((2, HEAD_WIDTH, tq), _F32)
    return pl.pallas_call(
        kern,
        grid=(n_tiles + 2,),
        in_specs=[
            pl.BlockSpec(memory_space=pltpu.SMEM),
            pl.BlockSpec((4, HEAD_DIM), lambda s: (0, 0)),
            pl.BlockSpec((HEAD_WIDTH, 1), lambda s: (0, 0)),
            pl.BlockSpec((1, 1, tq, HEAD_WIDTH), q_index),
            pl.BlockSpec((1, 1, seq, HEAD_WIDTH), kv_index(1)),
            pl.BlockSpec((1, 1, seq, HEAD_WIDTH), kv_index(2)),
        ],
        out_specs=pl.BlockSpec((tq, HEAD_WIDTH), out_index),
        out_shape=jax.ShapeDtypeStruct((T, n_heads * HEAD_WIDTH), _BF16),
        scratch_shapes=[
            pltpu.VMEM((2, n_kb, tk, HEAD_WIDTH), _BF16),
            pltpu.VMEM((2, n_kb, HEAD_WIDTH, tk), _BF16),
            pltpu.VMEM((tk // tq, tk, tq), _F32),
            scores, scores, col_stat, col_stat,
            acc, acc, col_stat, col_stat,
        ],
        compiler_params=_params("arbitrary"),
        name="diff_attn",
    )(slopes, lam_params, gain_col, qkv, qkv, qkv)


def _merge_kernel(o_ref, pz_ref, pzp_ref, pzn_ref, gz_ref, x_ref, bg_ref, pw_ref, ps_ref,
                  wa_ref, wp_ref, wo_ref, g_ref, out_ref, ext_ref, yp_ref, *, tm, seq):
    D = x_ref.shape[1]
    gdim = pw_ref.shape[1]
    rows = tm + 2 * HALO
    t0 = (pl.program_id(0) * tm) % seq
    prev = pzp_ref[...].astype(_F32)
    nxt = pzn_ref[...].astype(_F32)
    ext_ref[0:HALO] = jnp.where(t0 == 0, jnp.zeros_like(prev), prev)
    ext_ref[HALO:HALO + tm] = pz_ref[...].astype(_F32)
    ext_ref[HALO + tm:rows] = jnp.where(t0 + tm == seq, jnp.zeros_like(nxt), nxt)

    def ahead(a, k):
        return pltpu.roll(a, rows - k, axis=0)

    def behind(a, k):
        return pltpu.roll(a, k, axis=0)

    tpos = t0 + lax.broadcasted_iota(jnp.int32, (tm, 1), 0)
    for g, w in enumerate(POOL_WINDOWS):
        half = w // 2
        cols = slice(g * gdim, (g + 1) * gdim)
        e = ext_ref[:, cols]
        run, n = e, 1
        while n < half:
            run = run + ahead(run, n)
            n *= 2
        if half % F32_SUBLANES == 0:
            win = run[HALO - half:HALO - half + tm] + run[HALO:HALO + tm]
        else:
            win = (behind(run, half) + run)[HALO:HALO + tm]
        cnt = (jnp.minimum(tpos + half, seq) - jnp.maximum(tpos - half, 0)).astype(_F32)
        mixed = win * (1.0 / cnt) - e[HALO:HALO + tm]
        yg = jnp.dot(mixed.astype(_BF16), pw_ref[g], preferred_element_type=_F32)
        yp_ref[:, cols] = (yg * ps_ref[:, cols]).astype(_BF16)

    y_attn = jnp.dot(o_ref[...], wa_ref[...], preferred_element_type=_F32)
    y_pool = jnp.dot(yp_ref[...], wp_ref[...], preferred_element_type=_F32)
    gates = jax.nn.sigmoid(gz_ref[...].astype(_F32) + bg_ref[...])
    merged = gates[:, :D] * y_attn + gates[:, D:] * y_pool
    mo = jnp.dot(merged.astype(_BF16), wo_ref[...], preferred_element_type=_F32)
    out_ref[...] = x_ref[...] + _rms(mo, g_ref[...], NORM_EPS)


def _const_spec(shape):
    nd = len(shape)
    return pl.BlockSpec(shape, lambda i: (0,) * nd, pipeline_mode=pl.Buffered(1))


def _merge(o, pz, gz, x2, b_gate, pool_w, pool_scale, w_attn, w_pool, w_out, g_post, *, seq, tm):
    T, D = x2.shape
    aw = o.shape[1]
    hb = tm // HALO
    n_hb = T // HALO
    return pl.pallas_call(
        functools.partial(_merge_kernel, tm=tm, seq=seq),
        grid=(T // tm,),
        in_specs=[
            pl.BlockSpec((tm, aw), lambda i: (i, 0)),
            pl.BlockSpec((tm, aw), lambda i: (i, 0)),
            pl.BlockSpec((HALO, aw), lambda i: (jnp.maximum(i * hb - 1, 0), 0)),
            pl.BlockSpec((HALO, aw), lambda i: (jnp.minimum((i + 1) * hb, n_hb - 1), 0)),
            pl.BlockSpec((tm, 2 * D), lambda i: (i, 0)),
            pl.BlockSpec((tm, D), lambda i: (i, 0)),
            _const_spec(b_gate.shape),
            _const_spec(pool_w.shape),
            _const_spec(pool_scale.shape),
            _const_spec(w_attn.shape),
            _const_spec(w_pool.shape),
            _const_spec(w_out.shape),
            _const_spec(g_post.shape),
        ],
        out_specs=pl.BlockSpec((tm, D), lambda i: (i, 0)),
        out_shape=jax.ShapeDtypeStruct((T, D), _F32),
        scratch_shapes=[
            pltpu.VMEM((tm + 2 * HALO, aw), _F32),
            pltpu.VMEM((tm, aw), _BF16),
        ],
        compiler_params=_params("arbitrary"),
        name="merge",
    )(o, pz, pz, pz, gz, x2, b_gate, pool_w, pool_scale, w_attn, w_pool, w_out, g_post)


def _ffn_kernel(x_ref, xp_ref, xn_ref, g_ref, wg_ref, wv_ref, cwg_ref, cwv_ref, cbg_ref, cbv_ref,
                wd_ref, gp_ref, out_ref, h_ref, acc_ref, *, tm, seq):
    c = pl.program_id(1)
    rows = tm + 2 * HALO

    @pl.when(c == 0)
    def _():
        t0 = (pl.program_id(0) * tm) % seq
        g = g_ref[...]
        hp = _rms(xp_ref[...], g, NORM_EPS)
        hn = _rms(xn_ref[...], g, NORM_EPS)
        h_ref[0:HALO] = jnp.where(t0 == 0, jnp.zeros_like(hp), hp).astype(_BF16)
        h_ref[HALO:HALO + tm] = _rms(x_ref[...], g, NORM_EPS).astype(_BF16)
        h_ref[HALO + tm:rows] = jnp.where(t0 + tm == seq, jnp.zeros_like(hn), hn).astype(_BF16)
        acc_ref[...] = jnp.zeros_like(acc_ref)

    h = h_ref[...]

    def conv(w_ref, cw_ref, cb_ref):
        u = jnp.dot(h, w_ref[...], preferred_element_type=_F32)
        cw = cw_ref[...]
        u_prev = pltpu.roll(u, 1, axis=0)[HALO:HALO + tm]
        u_next = pltpu.roll(u, rows - 1, axis=0)[HALO:HALO + tm]
        return u_prev * cw[0:1] + u[HALO:HALO + tm] * cw[1:2] + u_next * cw[2:3] + cb_ref[...]

    gate = conv(wg_ref, cwg_ref, cbg_ref)
    val = conv(wv_ref, cwv_ref, cbv_ref)
    act = (jax.nn.gelu(gate) * val).astype(_BF16)
    acc_ref[...] += jnp.dot(act, wd_ref[...], preferred_element_type=_F32)

    @pl.when(c == pl.num_programs(1) - 1)
    def _():
        out_ref[...] = x_ref[...] + _rms(acc_ref[...], gp_ref[...], NORM_EPS)


def _ffn(x1, g_pre, w_up, conv_w, conv_b, w_down, g_post, *, seq, tm, tf):
    T, D = x1.shape
    d_ff = w_down.shape[0]
    n_c = d_ff // tf
    hb = tm // HALO
    n_hb = T // HALO
    return pl.pallas_call(
        functools.partial(_ffn_kernel, tm=tm, seq=seq),
        grid=(T // tm, n_c),
        in_specs=[
            pl.BlockSpec((tm, D), lambda i, c: (i, 0)),
            pl.BlockSpec((HALO, D), lambda i, c: (jnp.maximum(i * hb - 1, 0), 0)),
            pl.BlockSpec((HALO, D), lambda i, c: (jnp.minimum((i + 1) * hb, n_hb - 1), 0)),
            pl.BlockSpec((1, D), lambda i, c: (0, 0)),
            pl.BlockSpec((D, tf), lambda i, c: (0, c)),
            pl.BlockSpec((D, tf), lambda i, c: (0, n_c + c)),
            pl.BlockSpec((CONV_WIDTH, tf), lambda i, c: (0, c)),
            pl.BlockSpec((CONV_WIDTH, tf), lambda i, c: (0, n_c + c)),
            pl.BlockSpec((1, tf), lambda i, c: (0, c)),
            pl.BlockSpec((1, tf), lambda i, c: (0, n_c + c)),
            pl.BlockSpec((tf, D), lambda i, c: (c, 0)),
            pl.BlockSpec((1, D), lambda i, c: (0, 0)),
        ],
        out_specs=pl.BlockSpec((tm, D), lambda i, c: (i, 0)),
        out_shape=jax.ShapeDtypeStruct((T, D), _F32),
        scratch_shapes=[
            pltpu.VMEM((tm + 2 * HALO, D), _BF16),
            pltpu.VMEM((tm, D), _F32),
        ],
        compiler_params=_params("arbitrary", "arbitrary"),
        name="conv_ffn",
    )(x1, x1, x1, g_pre, w_up, w_up, conv_w, conv_w, conv_b, conv_b, w_down, g_post)


def _pick(n, pref):
    t = min(n, pref)
    while n % t:
        t //= 2
    return t


def kernel(x, g_pre_mix, w_in, b_gate, lambda_q1, lambda_k1, lambda_q2, lambda_k2, subln_gain,
           w_attn_branch, pool_w, pool_scale, w_pool_branch, w_out, g_post_mix, g_pre_ffn, w_up,
           conv_w, conv_b, w_down, g_post_ffn):
    B, S, D = x.shape
    depth = w_in.shape[0]
    aw = w_attn_branch.shape[1]
    n_heads = aw // HEAD_WIDTH
    d_ff = w_down.shape[1]
    T = B * S
    slopes_np = np.array([2.0 ** (-8.0 * (h + 1) / n_heads) for h in range(n_heads)], dtype=np.float32)
    assert np.all(np.frexp(slopes_np)[0] == 0.5)
    slopes = jnp.asarray(slopes_np)
    row = lambda a: a.reshape(1, -1).astype(_F32)

    x2 = x.reshape(T, D)
    for l in range(depth):
        lambda_init = 0.8 - 0.6 * math.exp(-0.3 * l)
        qkv, pz, gz = _in_proj(x2, row(g_pre_mix[l]), w_in[l].astype(_BF16),
                               n_heads=n_heads, tm=_pick(T, 1024))
        lam_params = jnp.stack([lambda_q1[l], lambda_k1[l], lambda_q2[l], lambda_k2[l]]).astype(_F32)
        o = _attention(qkv, slopes, lam_params, subln_gain[l].reshape(-1, 1).astype(_F32),
                       batch=B, seq=S, tq=_pick(S, 256), tk=_pick(S, 512), lambda_init=lambda_init)
        x2 = _merge(o, pz, gz, x2, row(b_gate[l]), pool_w[l].astype(_BF16), row(pool_scale[l]),
                    w_attn_branch[l].astype(_BF16), w_pool_branch[l].astype(_BF16),
                    w_out[l].astype(_BF16), row(g_post_mix[l]), seq=S, tm=_pick(S, 256))
        x2 = _ffn(x2, row(g_pre_ffn[l]), w_up[l].astype(_BF16), conv_w[l].astype(_F32),
                  row(conv_b[l]), w_down[l].astype(_BF16), row(g_post_ffn[l]),
                  seq=S, tm=_pick(S, 512), tf=_pick(d_ff, 512))
    return x2.reshape(B, S, D)
```

```python
import functools
import math

import jax
import jax.numpy as jnp
import numpy as np
from jax import lax
from jax.experimental import pallas as pl
from jax.experimental.pallas import tpu as pltpu

HEAD_DIM = 64
HEAD_WIDTH = 2 * HEAD_DIM
POOL_WINDOWS = (2, 4, 8, 16)
CONV_WIDTH = 3
NORM_EPS = 1e-6
SUBLN_EPS = 1e-5
HALO = 16
F32_SUBLANES = 8
MXU_COLS = 256
FFN_ROW_GROUPS = 4
POS_RADIX = 64
LOG2E = math.log2(math.e)
V7X_VMEM_LIMIT_BYTES = 56 * 1024 * 1024

_F32 = jnp.float32
_BF16 = jnp.bfloat16


def _rms(x, g, eps):
    return x * lax.rsqrt(jnp.mean(x * x, axis=-1, keepdims=True) + eps) * g


def _params(*sem):
    return pltpu.CompilerParams(dimension_semantics=sem, vmem_limit_bytes=V7X_VMEM_LIMIT_BYTES)


def _in_proj_kernel(x_ref, g_ref, w_ref, qkv_ref, pz_ref, gz_ref, h_ref, *, n_heads):
    j = pl.program_id(1)

    @pl.when(j == 0)
    def _():
        h_ref[...] = _rms(x_ref[...], g_ref[...], NORM_EPS).astype(_BF16)

    def project(store):
        h = h_ref[...]
        for lo in range(0, w_ref.shape[1], MXU_COLS):
            z = jnp.dot(h, w_ref[:, lo:lo + MXU_COLS], preferred_element_type=_F32)
            store(lo, z.astype(_BF16))

    def store_heads(lo, z):
        for k in range(MXU_COLS // HEAD_WIDTH):
            qkv_ref[0, lo // HEAD_WIDTH + k] = z[:, k * HEAD_WIDTH:(k + 1) * HEAD_WIDTH]

    def store_pz(lo, z):
        pz_ref[:, lo:lo + MXU_COLS] = z

    def store_gz(lo, z):
        gz_ref[:, lo:lo + MXU_COLS] = z

    @pl.when(j < 3)
    def _():
        project(store_heads)

    @pl.when(j == 3)
    def _():
        project(store_pz)

    @pl.when(j > 3)
    def _():
        project(store_gz)


def _in_proj(x2, g, w, *, n_heads, tm):
    T, D = x2.shape
    aw = n_heads * HEAD_WIDTH
    n_gate_blocks = 2 * D // aw
    grid = (T // tm, 4 + n_gate_blocks)
    return pl.pallas_call(
        functools.partial(_in_proj_kernel, n_heads=n_heads),
        grid=grid,
        in_specs=[
            pl.BlockSpec((tm, D), lambda i, j: (i, 0)),
            pl.BlockSpec((1, D), lambda i, j: (0, 0)),
            pl.BlockSpec((D, aw), lambda i, j: (0, j)),
        ],
        out_specs=[
            pl.BlockSpec((1, n_heads, tm, HEAD_WIDTH), lambda i, j: (jnp.minimum(j, 2), 0, i, 0)),
            pl.BlockSpec((tm, aw), lambda i, j: (i, 0)),
            pl.BlockSpec((tm, aw), lambda i, j: (i, jnp.maximum(j - 4, 0))),
        ],
        out_shape=[
            jax.ShapeDtypeStruct((3, n_heads, T, HEAD_WIDTH), _BF16),
            jax.ShapeDtypeStruct((T, aw), _BF16),
            jax.ShapeDtypeStruct((T, 2 * D), _BF16),
        ],
        scratch_shapes=[pltpu.VMEM((tm, D), _BF16)],
        compiler_params=_params("arbitrary", "arbitrary"),
        name="in_proj",
    )(x2, g, w)


def _pos_features(pos, idx, base, values):
    hi = (pos // POS_RADIX).astype(_F32)
    lo = (pos % POS_RADIX).astype(_F32)
    out = jnp.zeros(pos.shape, _F32)
    for n, val in enumerate(values(hi, lo)):
        out = jnp.where(idx == base + n, val, out)
    return out


def _attn_kernel(slopes_ref, lam_ref, gain_ref, q_ref, k_ref, v_ref, o_ref,
                 kaug_ref, vt_ref, corr_ref, s_even_ref, s_odd_ref, m_even_ref, m_odd_ref,
                 acc_even_ref, acc_odd_ref, l_even_ref, l_odd_ref,
                 *, tq, tk, n_kb, n_q, n_heads, n_tiles, lambda_init):
    step = pl.program_id(0)
    tile = jnp.minimum(step, n_tiles - 1)
    head = tile // n_q
    qi = tile % n_q
    prev_head = jnp.minimum(jnp.maximum(step - 1, 0), n_tiles - 1) // n_q
    feat_base = (HEAD_DIM, 0)

    @pl.when(step == 0)
    def _():
        s_odd_ref[...] = jnp.zeros_like(s_odd_ref)
        m_odd_ref[...] = jnp.zeros_like(m_odd_ref)
        acc_odd_ref[...] = jnp.zeros_like(acc_odd_ref)
        l_odd_ref[...] = jnp.ones_like(l_odd_ref)

    @pl.when((qi == 0) & (step < n_tiles))
    def _():
        slope = slopes_ref[head % n_heads]
        lane = lax.broadcasted_iota(jnp.int32, (tk, HEAD_WIDTH), 1)
        row = lax.broadcasted_iota(jnp.int32, (tk, HEAD_WIDTH), 0)
        feats, feats_per_block, own_half = [], [], []
        for c in range(2):
            feats.append(_pos_features(
                row, lane, feat_base[c],
                lambda hi, lo: (-POS_RADIX * slope, -slope, POS_RADIX * slope * hi, slope * lo)))
            feats_per_block.append(jnp.where(lane == feat_base[c] + 2, slope * tk, 0.0))
            own_half.append((lane >= c * HEAD_DIM) & (lane < (c + 1) * HEAD_DIM))
        for jb in range(n_kb):
            kb = k_ref[0, 0, jb * tk:(jb + 1) * tk, :].astype(_F32)
            for c in range(2):
                kaug_ref[c, jb] = jnp.where(own_half[c], kb, feats[c]).astype(_BF16)
                feats[c] = feats[c] + feats_per_block[c]
            vb = v_ref[0, 0, jb * tk:(jb + 1) * tk, :].astype(_F32)
            vt_ref[head % 2, jb] = vb.T.astype(_BF16)
        rel = (lax.broadcasted_iota(jnp.int32, (tk, tq), 0)
               - lax.broadcasted_iota(jnp.int32, (tk, tq), 1))
        for v in range(tk // tq):
            corr_ref[v] = (2.0 * slope) * jnp.maximum(rel - v * tq, 0).astype(_F32)

    q_t = (q_ref[0, 0].astype(_F32) * (HEAD_DIM ** -0.5)).T
    q_half = (q_t[:HEAD_DIM], q_t[HEAD_DIM:])
    sub = lax.broadcasted_iota(jnp.int32, (F32_SUBLANES, tq), 0)
    qfeat = _pos_features(qi * tq + lax.broadcasted_iota(jnp.int32, (F32_SUBLANES, tq), 1),
                          sub, 0, lambda hi, lo: (hi, lo, 1.0, 1.0))
    no_feat = jnp.zeros((HEAD_DIM - F32_SUBLANES, tq), _F32)

    def with_features(c, f):
        other_half = jnp.concatenate([f, no_feat], axis=0)
        parts = (q_half[0], other_half) if c == 0 else (other_half, q_half[1])
        return jnp.concatenate(parts, axis=0).astype(_BF16)

    q_left = [with_features(c, qfeat) for c in range(2)]
    q_right = [with_features(c, -qfeat) for c in range(2)]

    jd = (qi * tq) // tk
    correction = corr_ref[(qi * tq - jd * tk) // tq]

    lam_p = lam_ref[...]
    lam = (jnp.exp(jnp.sum(lam_p[0:1] * lam_p[1:2], axis=-1, keepdims=True))
           - jnp.exp(jnp.sum(lam_p[2:3] * lam_p[3:4], axis=-1, keepdims=True))
           + lambda_init)

    def run_step(s_new_ref, m_new_ref, acc_new_ref, l_new_ref,
                 s_old_ref, m_old_ref, acc_old_ref, l_old_ref):
        l_old = [jnp.sum(l_old_ref[c], axis=0, keepdims=True) for c in range(2)]
        o_t = acc_old_ref[0] / l_old[0] - lam * (acc_old_ref[1] / l_old[1])
        ms = jnp.mean(o_t * o_t, axis=0, keepdims=True)
        o_t = o_t * lax.rsqrt(ms + SUBLN_EPS) * gain_ref[...] * (1.0 - lambda_init)
        o_ref[...] = o_t.T.astype(o_ref.dtype)

        def score_block(d, col_max):
            jb = jd if d == 0 else lax.rem(jd + d, n_kb)
            out = []
            for c in range(2):
                q_c = q_left[c] if d == 0 else jnp.where(jb < jd, q_left[c], q_right[c])
                s = jnp.dot(kaug_ref[c, jb], q_c, preferred_element_type=_F32)
                if d == 0:
                    s = s - correction
                s = s * LOG2E
                s_new_ref[c, jb] = s
                block_max = jnp.max(s.reshape(tk // 8, 8, tq), axis=0)
                out.append(block_max if col_max is None else jnp.maximum(col_max[c], block_max))
            return out

        m_old = [jnp.max(m_old_ref[c], axis=0, keepdims=True) for c in range(2)]
        col_sum = [jnp.zeros((8, tq), _F32), jnp.zeros((8, tq), _F32)]
        col_max = None
        for jb in range(n_kb):
            col_max = score_block(jb, col_max)
            v_t = vt_ref[prev_head % 2, jb]
            for c in range(2):
                p = jnp.exp2(s_old_ref[c, jb] - m_old[c])
                col_sum[c] = col_sum[c] + jnp.sum(p.reshape(tk // 8, 8, tq), axis=0)
                pv = jnp.dot(v_t, p.astype(_BF16), preferred_element_type=_F32)
                if jb == 0:
                    acc_new_ref[c] = pv
                else:
                    acc_new_ref[c] += pv
        for c in range(2):
            m_new_ref[c] = col_max[c]
            l_new_ref[c] = col_sum[c]

    even = (s_even_ref, m_even_ref, acc_even_ref, l_even_ref)
    odd = (s_odd_ref, m_odd_ref, acc_odd_ref, l_odd_ref)

    @pl.when(step % 2 == 0)
    def _():
        run_step(*even, *odd)

    @pl.when(step % 2 == 1)
    def _():
        run_step(*odd, *even)


def _attention(qkv, slopes, lam_params, gain_col, *, batch, seq, tq, tk, lambda_init):
    _, n_heads, T, _ = qkv.shape
    n_q = seq // tq
    n_kb = seq // tk
    n_tiles = batch * n_heads * n_q
    assert tk % tq == 0 and seq <= POS_RADIX * 256
    kern = functools.partial(_attn_kernel, tq=tq, tk=tk, n_kb=n_kb, n_q=n_q, n_heads=n_heads,
                             n_tiles=n_tiles, lambda_init=lambda_init)

    def score_tile(s):
        t = jnp.minimum(s, n_tiles - 1)
        return t // (n_heads * n_q), (t // n_q) % n_heads, t % n_q

    def output_tile(s):
        return score_tile(jnp.maximum(s - 2, 0))

    def q_index(s):
        b, h, i = score_tile(s)
        return 0, h, b * n_q + i, 0

    def kv_index(which):
        def index(s):
            b, h, _ = score_tile(s)
            return which, h, b, 0
        return index

    def out_index(s):
        b, h, i = output_tile(s)
        return b * n_q + i, h

    scores = pltpu.VMEM((2, n_kb, tk, tq), _F32)
    col_stat = pltpu.VMEM((2, F32_SUBLANES, tq), _F32)
    acc = pltpu.VMEM((2, HEAD_WIDTH, tq), _F32)
    return pl.pallas_call(
        kern,
        grid=(n_tiles + 2,),
        in_specs=[
            pl.BlockSpec(memory_space=pltpu.SMEM),
            pl.BlockSpec((4, HEAD_DIM), lambda s: (0, 0)),
            pl.BlockSpec((HEAD_WIDTH, 1), lambda s: (0, 0)),
            pl.BlockSpec((1, 1, tq, HEAD_WIDTH), q_index),
            pl.BlockSpec((1, 1, seq, HEAD_WIDTH), kv_index(1)),
            pl.BlockSpec((1, 1, seq, HEAD_WIDTH), kv_index(2)),
        ],
        out_specs=pl.BlockSpec((tq, HEAD_WIDTH), out_index),
        out_shape=jax.ShapeDtypeStruct((T, n_heads * HEAD_WIDTH), _BF16),
        scratch_shapes=[
            pltpu.VMEM((2, n_kb, tk, HEAD_WIDTH), _BF16),
            pltpu.VMEM((2, n_kb, HEAD_WIDTH, tk), _BF16),
            pltpu.VMEM((tk // tq, tk, tq), _F32),
            scores, scores, col_stat, col_stat,
            acc, acc, col_stat, col_stat,
        ],
        compiler_params=_params("arbitrary"),
        name="diff_attn",
    )(slopes, lam_params, gain_col, qkv, qkv, qkv)


def _merge_kernel(o_ref, pz_ref, pzp_ref, pzn_ref, gz_ref, x_ref, bg_ref, pw_ref, ps_ref,
                  wa_ref, wp_ref, wo_ref, g_ref, out_ref, ext_ref, yp_ref, *, tm, seq):
    D = x_ref.shape[1]
    gdim = pw_ref.shape[1]
    rows = tm + 2 * HALO
    t0 = (pl.program_id(0) * tm) % seq
    prev = pzp_ref[...].astype(_F32)
    nxt = pzn_ref[...].astype(_F32)
    ext_ref[0:HALO] = jnp.where(t0 == 0, jnp.zeros_like(prev), prev)
    ext_ref[HALO:HALO + tm] = pz_ref[...].astype(_F32)
    ext_ref[HALO + tm:rows] = jnp.where(t0 + tm == seq, jnp.zeros_like(nxt), nxt)

    def ahead(a, k):
        return pltpu.roll(a, rows - k, axis=0)

    def behind(a, k):
        return pltpu.roll(a, k, axis=0)

    tpos = t0 + lax.broadcasted_iota(jnp.int32, (tm, 1), 0)
    for g, w in enumerate(POOL_WINDOWS):
        half = w // 2
        cols = slice(g * gdim, (g + 1) * gdim)
        e = ext_ref[:, cols]
        run, n = e, 1
        while n < half:
            run = run + ahead(run, n)
            n *= 2
        if half % F32_SUBLANES == 0:
            win = run[HALO - half:HALO - half + tm] + run[HALO:HALO + tm]
        else:
            win = (behind(run, half) + run)[HALO:HALO + tm]
        cnt = (jnp.minimum(tpos + half, seq) - jnp.maximum(tpos - half, 0)).astype(_F32)
        mixed = win * (1.0 / cnt) - e[HALO:HALO + tm]
        yg = jnp.dot(mixed.astype(_BF16), pw_ref[g], preferred_element_type=_F32)
        yp_ref[:, cols] = (yg * ps_ref[:, cols]).astype(_BF16)

    y_attn = jnp.dot(o_ref[...], wa_ref[...], preferred_element_type=_F32)
    y_pool = jnp.dot(yp_ref[...], wp_ref[...], preferred_element_type=_F32)
    gates = jax.nn.sigmoid(gz_ref[...].astype(_F32) + bg_ref[...])
    merged = gates[:, :D] * y_attn + gates[:, D:] * y_pool
    mo = jnp.dot(merged.astype(_BF16), wo_ref[...], preferred_element_type=_F32)
    out_ref[...] = x_ref[...] + _rms(mo, g_ref[...], NORM_EPS)


def _const_spec(shape):
    nd = len(shape)
    return pl.BlockSpec(shape, lambda i: (0,) * nd, pipeline_mode=pl.Buffered(1))


def _merge(o, pz, gz, x2, b_gate, pool_w, pool_scale, w_attn, w_pool, w_out, g_post, *, seq, tm):
    T, D = x2.shape
    aw = o.shape[1]
    hb = tm // HALO
    n_hb = T // HALO
    return pl.pallas_call(
        functools.partial(_merge_kernel, tm=tm, seq=seq),
        grid=(T // tm,),
        in_specs=[
            pl.BlockSpec((tm, aw), lambda i: (i, 0)),
            pl.BlockSpec((tm, aw), lambda i: (i, 0)),
            pl.BlockSpec((HALO, aw), lambda i: (jnp.maximum(i * hb - 1, 0), 0)),
            pl.BlockSpec((HALO, aw), lambda i: (jnp.minimum((i + 1) * hb, n_hb - 1), 0)),
            pl.BlockSpec((tm, 2 * D), lambda i: (i, 0)),
            pl.BlockSpec((tm, D), lambda i: (i, 0)),
            _const_spec(b_gate.shape),
            _const_spec(pool_w.shape),
            _const_spec(pool_scale.shape),
            _const_spec(w_attn.shape),
            _const_spec(w_pool.shape),
            _const_spec(w_out.shape),
            _const_spec(g_post.shape),
        ],
        out_specs=pl.BlockSpec((tm, D), lambda i: (i, 0)),
        out_shape=jax.ShapeDtypeStruct((T, D), _F32),
        scratch_shapes=[
            pltpu.VMEM((tm + 2 * HALO, aw), _F32),
            pltpu.VMEM((tm, aw), _BF16),
        ],
        compiler_params=_params("arbitrary"),
        name="merge",
    )(o, pz, pz, pz, gz, x2, b_gate, pool_w, pool_scale, w_attn, w_pool, w_out, g_post)


def _ffn_kernel(x_ref, xp_ref, xn_ref, g_ref, wg_ref, wv_ref, cwg_ref, cwv_ref, cbg_ref, cbv_ref,
                wd_ref, gp_ref, out_ref, h_ref, acc_ref, *, tm, seq):
    c = pl.program_id(1)
    rows = tm + 2 * HALO

    @pl.when(c == 0)
    def _():
        t0 = (pl.program_id(0) * tm) % seq
        g = g_ref[...]
        hp = _rms(xp_ref[...], g, NORM_EPS)
        hn = _rms(xn_ref[...], g, NORM_EPS)
        h_ref[0:HALO] = jnp.where(t0 == 0, jnp.zeros_like(hp), hp).astype(_BF16)
        h_ref[HALO:HALO + tm] = _rms(x_ref[...], g, NORM_EPS).astype(_BF16)
        h_ref[HALO + tm:rows] = jnp.where(t0 + tm == seq, jnp.zeros_like(hn), hn).astype(_BF16)
        acc_ref[...] = jnp.zeros_like(acc_ref)

    h = h_ref[...]
    n_tiles = rows // HALO
    row_cuts = [HALO * (n_tiles * k // FFN_ROW_GROUPS) for k in range(FFN_ROW_GROUPS + 1)]

    def conv(w_ref, cw_ref, cb_ref):
        w = w_ref[...]
        u = jnp.concatenate([jnp.dot(h[lo:hi], w, preferred_element_type=_F32)
                             for lo, hi in zip(row_cuts[:-1], row_cuts[1:])], axis=0)
        cw = cw_ref[...]
        u_prev = pltpu.roll(u, 1, axis=0)[HALO:HALO + tm]
        u_next = pltpu.roll(u, rows - 1, axis=0)[HALO:HALO + tm]
        return u_prev * cw[0:1] + u[HALO:HALO + tm] * cw[1:2] + u_next * cw[2:3] + cb_ref[...]

    gate = conv(wg_ref, cwg_ref, cbg_ref)
    val = conv(wv_ref, cwv_ref, cbv_ref)
    act = (jax.nn.gelu(gate) * val).astype(_BF16)
    acc_ref[...] += jnp.dot(act, wd_ref[...], preferred_element_type=_F32)

    @pl.when(c == pl.num_programs(1) - 1)
    def _():
        out_ref[...] = x_ref[...] + _rms(acc_ref[...], gp_ref[...], NORM_EPS)


def _ffn(x1, g_pre, w_up, conv_w, conv_b, w_down, g_post, *, seq, tm, tf):
    T, D = x1.shape
    d_ff = w_down.shape[0]
    n_c = d_ff // tf
    hb = tm // HALO
    n_hb = T // HALO
    return pl.pallas_call(
        functools.partial(_ffn_kernel, tm=tm, seq=seq),
        grid=(T // tm, n_c),
        in_specs=[
            pl.BlockSpec((tm, D), lambda i, c: (i, 0)),
            pl.BlockSpec((HALO, D), lambda i, c: (jnp.maximum(i * hb - 1, 0), 0)),
            pl.BlockSpec((HALO, D), lambda i, c: (jnp.minimum((i + 1) * hb, n_hb - 1), 0)),
            pl.BlockSpec((1, D), lambda i, c: (0, 0)),
            pl.BlockSpec((D, tf), lambda i, c: (0, c)),
            pl.BlockSpec((D, tf), lambda i, c: (0, n_c + c)),
            pl.BlockSpec((CONV_WIDTH, tf), lambda i, c: (0, c)),
            pl.BlockSpec((CONV_WIDTH, tf), lambda i, c: (0, n_c + c)),
            pl.BlockSpec((1, tf), lambda i, c: (0, c)),
            pl.BlockSpec((1, tf), lambda i, c: (0, n_c + c)),
            pl.BlockSpec((tf, D), lambda i, c: (c, 0)),
            pl.BlockSpec((1, D), lambda i, c: (0, 0)),
        ],
        out_specs=pl.BlockSpec((tm, D), lambda i, c: (i, 0)),
        out_shape=jax.ShapeDtypeStruct((T, D), _F32),
        scratch_shapes=[
            pltpu.VMEM((tm + 2 * HALO, D), _BF16),
            pltpu.VMEM((tm, D), _F32),
        ],
        compiler_params=_params("arbitrary", "arbitrary"),
        name="conv_ffn",
    )(x1, x1, x1, g_pre, w_up, w_up, conv_w, conv_w, conv_b, conv_b, w_down, g_post)


def _pick(n, pref):
    t = min(n, pref)
    while n % t:
        t //= 2
    return t


def kernel(x, g_pre_mix, w_in, b_gate, lambda_q1, lambda_k1, lambda_q2, lambda_k2, subln_gain,
           w_attn_branch, pool_w, pool_scale, w_pool_branch, w_out, g_post_mix, g_pre_ffn, w_up,
           conv_w, conv_b, w_down, g_post_ffn):
    B, S, D = x.shape
    depth = w_in.shape[0]
    aw = w_attn_branch.shape[1]
    n_heads = aw // HEAD_WIDTH
    d_ff = w_down.shape[1]
    T = B * S
    slopes_np = np.array([2.0 ** (-8.0 * (h + 1) / n_heads) for h in range(n_heads)], dtype=np.float32)
    assert np.all(np.frexp(slopes_np)[0] == 0.5)
    slopes = jnp.asarray(slopes_np)
    row = lambda a: a.reshape(1, -1).astype(_F32)

    x2 = x.reshape(T, D)
    for l in range(depth):
        lambda_init = 0.8 - 0.6 * math.exp(-0.3 * l)
        qkv, pz, gz = _in_proj(x2, row(g_pre_mix[l]), w_in[l].astype(_BF16),
                               n_heads=n_heads, tm=_pick(T, 1024))
        lam_params = jnp.stack([lambda_q1[l], lambda_k1[l], lambda_q2[l], lambda_k2[l]]).astype(_F32)
        o = _attention(qkv, slopes, lam_params, subln_gain[l].reshape(-1, 1).astype(_F32),
                       batch=B, seq=S, tq=_pick(S, 256), tk=_pick(S, 512), lambda_init=lambda_init)
        x2 = _merge(o, pz, gz, x2, row(b_gate[l]), pool_w[l].astype(_BF16), row(pool_scale[l]),
                    w_attn_branch[l].astype(_BF16), w_pool_branch[l].astype(_BF16),
                    w_out[l].astype(_BF16), row(g_post_mix[l]), seq=S, tm=_pick(S, 256))
        x2 = _ffn(x2, row(g_pre_ffn[l]), w_up[l].astype(_BF16), conv_w[l].astype(_F32),
                  row(conv_b[l]), w_down[l].astype(_BF16), row(g_post_ffn[l]),
                  seq=S, tm=_pick(S, 512), tf=_pick(d_ff, 512))
    return x2.reshape(B, S, D)
```

```python
import functools
import math

import jax
import jax.numpy as jnp
import numpy as np
from jax import lax
from jax.experimental import pallas as pl
from jax.experimental.pallas import tpu as pltpu

HEAD_DIM = 64
HEAD_WIDTH = 2 * HEAD_DIM
POOL_WINDOWS = (2, 4, 8, 16)
CONV_WIDTH = 3
NORM_EPS = 1e-6
SUBLN_EPS = 1e-5
HALO = 16
F32_SUBLANES = 8
MXU_COLS = 256
POS_RADIX = 64
LOG2E = math.log2(math.e)
V7X_VMEM_LIMIT_BYTES = 56 * 1024 * 1024

_F32 = jnp.float32
_BF16 = jnp.bfloat16


def _rms(x, g, eps):
    return x * lax.rsqrt(jnp.mean(x * x, axis=-1, keepdims=True) + eps) * g


def _divmod(x, n):
    if n & (n - 1) == 0:
        return x >> (n.bit_length() - 1), x & (n - 1)
    return x // n, x % n


def _params(*sem):
    return pltpu.CompilerParams(dimension_semantics=sem, vmem_limit_bytes=V7X_VMEM_LIMIT_BYTES)


def _in_proj_kernel(x_ref, g_ref, w_ref, qt_ref, kv_ref, pz_ref, gz_ref, h_ref, *, n_heads):
    j = pl.program_id(1)

    @pl.when(j == 0)
    def _():
        h_ref[...] = _rms(x_ref[...], g_ref[...], NORM_EPS).astype(_BF16)

    def project(store):
        h = h_ref[...]
        for lo in range(0, w_ref.shape[1], MXU_COLS):
            store(lo, jnp.dot(h, w_ref[:, lo:lo + MXU_COLS], preferred_element_type=_F32))

    def heads_of(lo, z):
        for k in range(MXU_COLS // HEAD_WIDTH):
            yield lo // HEAD_WIDTH + k, z[:, k * HEAD_WIDTH:(k + 1) * HEAD_WIDTH]

    def store_q(lo, z):
        for head, z_h in heads_of(lo, z):
            qt_ref[head] = (z_h * (HEAD_DIM ** -0.5)).T.astype(_BF16)

    def store_kv(lo, z):
        for head, z_h in heads_of(lo, z):
            kv_ref[0, head] = z_h.astype(_BF16)

    def store_pz(lo, z):
        pz_ref[:, lo:lo + MXU_COLS] = z.astype(_BF16)

    def store_gz(lo, z):
        gz_ref[:, lo:lo + MXU_COLS] = z.astype(_BF16)

    @pl.when(j == 0)
    def _():
        project(store_q)

    @pl.when((j == 1) | (j == 2))
    def _():
        project(store_kv)

    @pl.when(j == 3)
    def _():
        project(store_pz)

    @pl.when(j > 3)
    def _():
        project(store_gz)


def _in_proj(x2, g, w, *, n_heads, tm):
    T, D = x2.shape
    aw = n_heads * HEAD_WIDTH
    n_gate_blocks = 2 * D // aw
    grid = (T // tm, 4 + n_gate_blocks)
    return pl.pallas_call(
        functools.partial(_in_proj_kernel, n_heads=n_heads),
        grid=grid,
        in_specs=[
            pl.BlockSpec((tm, D), lambda i, j: (i, 0)),
            pl.BlockSpec((1, D), lambda i, j: (0, 0)),
            pl.BlockSpec((D, aw), lambda i, j: (0, j)),
        ],
        out_specs=[
            pl.BlockSpec((n_heads, HEAD_WIDTH, tm), lambda i, j: (0, 0, i)),
            pl.BlockSpec((1, n_heads, tm, HEAD_WIDTH), lambda i, j: (jnp.clip(j - 1, 0, 1), 0, i, 0)),
            pl.BlockSpec((tm, aw), lambda i, j: (i, 0)),
            pl.BlockSpec((tm, aw), lambda i, j: (i, jnp.maximum(j - 4, 0))),
        ],
        out_shape=[
            jax.ShapeDtypeStruct((n_heads, HEAD_WIDTH, T), _BF16),
            jax.ShapeDtypeStruct((2, n_heads, T, HEAD_WIDTH), _BF16),
            jax.ShapeDtypeStruct((T, aw), _BF16),
            jax.ShapeDtypeStruct((T, 2 * D), _BF16),
        ],
        scratch_shapes=[pltpu.VMEM((tm, D), _BF16)],
        compiler_params=_params("arbitrary", "arbitrary"),
        name="in_proj",
    )(x2, g, w)


def _pos_features(pos, idx, base, values):
    hi = (pos // POS_RADIX).astype(_F32)
    lo = (pos % POS_RADIX).astype(_F32)
    out = jnp.zeros(pos.shape, _F32)
    for n, val in enumerate(values(hi, lo)):
        out = jnp.where(idx == base + n, val, out)
    return out


def _attn_kernel(slopes_ref, lam_ref, gain_ref, q_ref, k_ref, v_ref, o_ref,
                 kaug_ref, vt_ref, corr_ref, s_even_ref, s_odd_ref, m_even_ref, m_odd_ref,
                 acc_even_ref, acc_odd_ref, l_even_ref, l_odd_ref,
                 *, tq, tk, n_kb, n_q, n_heads, n_tiles, lambda_init):
    step = pl.program_id(0)
    tile = jnp.minimum(step, n_tiles - 1)
    head, qi = _divmod(tile, n_q)
    prev_head, _ = _divmod(jnp.minimum(jnp.maximum(step - 1, 0), n_tiles - 1), n_q)
    feat_base = (HEAD_DIM, 0)

    @pl.when(step == 0)
    def _():
        s_odd_ref[...] = jnp.zeros_like(s_odd_ref)
        m_odd_ref[...] = jnp.zeros_like(m_odd_ref)
        acc_odd_ref[...] = jnp.zeros_like(acc_odd_ref)
        l_odd_ref[...] = jnp.ones_like(l_odd_ref)

    @pl.when((qi == 0) & (step < n_tiles))
    def _():
        slope = slopes_ref[_divmod(head, n_heads)[1]]
        lane = lax.broadcasted_iota(jnp.int32, (tk, HEAD_WIDTH), 1)
        row = lax.broadcasted_iota(jnp.int32, (tk, HEAD_WIDTH), 0)
        feats, feats_per_block, own_half = [], [], []
        for c in range(2):
            feats.append(_pos_features(
                row, lane, feat_base[c],
                lambda hi, lo: (-POS_RADIX * slope, -slope, POS_RADIX * slope * hi, slope * lo)))
            feats_per_block.append(jnp.where(lane == feat_base[c] + 2, slope * tk, 0.0))
            own_half.append((lane >= c * HEAD_DIM) & (lane < (c + 1) * HEAD_DIM))
        for jb in range(n_kb):
            kb = k_ref[0, 0, jb * tk:(jb + 1) * tk, :].astype(_F32)
            for c in range(2):
                kaug_ref[c, jb] = jnp.where(own_half[c], kb, feats[c]).astype(_BF16)
                feats[c] = feats[c] + feats_per_block[c]
            vb = v_ref[0, 0, jb * tk:(jb + 1) * tk, :].astype(_F32)
            vt_ref[head % 2, jb] = vb.T.astype(_BF16)
        rel = (lax.broadcasted_iota(jnp.int32, (tk, tq), 0)
               - lax.broadcasted_iota(jnp.int32, (tk, tq), 1))
        for v in range(tk // tq):
            corr_ref[v] = (2.0 * slope) * jnp.maximum(rel - v * tq, 0).astype(_F32)

    q_t = q_ref[0]
    q_half = (q_t[:HEAD_DIM], q_t[HEAD_DIM:])
    sub = lax.broadcasted_iota(jnp.int32, (HALO, tq), 0)
    qfeat = _pos_features(qi * tq + lax.broadcasted_iota(jnp.int32, (HALO, tq), 1),
                          sub, 0, lambda hi, lo: (hi, lo, 1.0, 1.0)).astype(_BF16)
    no_feat = jnp.zeros((HEAD_DIM - HALO, tq), _BF16)

    def with_features(c, f):
        other_half = jnp.concatenate([f, no_feat], axis=0)
        parts = (q_half[0], other_half) if c == 0 else (other_half, q_half[1])
        return jnp.concatenate(parts, axis=0)

    q_left = [with_features(c, qfeat) for c in range(2)]
    q_right = [with_features(c, -qfeat) for c in range(2)]

    jd, offset_in_block = _divmod(qi * tq, tk)
    correction = corr_ref[_divmod(offset_in_block, tq)[0]]

    lam_p = lam_ref[...]
    lam = (jnp.exp(jnp.sum(lam_p[0:1] * lam_p[1:2], axis=-1, keepdims=True))
           - jnp.exp(jnp.sum(lam_p[2:3] * lam_p[3:4], axis=-1, keepdims=True))
           + lambda_init)

    def run_step(s_new_ref, m_new_ref, acc_new_ref, l_new_ref,
                 s_old_ref, m_old_ref, acc_old_ref, l_old_ref):
        l_old = [jnp.sum(l_old_ref[c], axis=0, keepdims=True) for c in range(2)]
        o_t = acc_old_ref[0] / l_old[0] - lam * (acc_old_ref[1] / l_old[1])
        ms = jnp.mean(o_t * o_t, axis=0, keepdims=True)
        o_t = o_t * lax.rsqrt(ms + SUBLN_EPS) * gain_ref[...] * (1.0 - lambda_init)
        o_ref[...] = o_t.T.astype(o_ref.dtype)

        def score_block(d, col_max):
            jb = jd if d == 0 else _divmod(jd + d, n_kb)[1]
            out = []
            for c in range(2):
                q_c = q_left[c] if d == 0 else jnp.where(jb < jd, q_left[c], q_right[c])
                s = jnp.dot(kaug_ref[c, jb], q_c, preferred_element_type=_F32)
                if d == 0:
                    s = s - correction
                s = s * LOG2E
                s_new_ref[c, jb] = s
                block_max = jnp.max(s.reshape(tk // 8, 8, tq), axis=0)
                out.append(block_max if col_max is None else jnp.maximum(col_max[c], block_max))
            return out

        m_old = [jnp.max(m_old_ref[c], axis=0, keepdims=True) for c in range(2)]
        col_sum = [jnp.zeros((8, tq), _F32), jnp.zeros((8, tq), _F32)]
        col_max = None
        for jb in range(n_kb):
            col_max = score_block(jb, col_max)
            v_t = vt_ref[prev_head % 2, jb]
            for c in range(2):
                p = jnp.exp2(s_old_ref[c, jb] - m_old[c])
                col_sum[c] = col_sum[c] + jnp.sum(p.reshape(tk // 8, 8, tq), axis=0)
                pv = jnp.dot(v_t, p.astype(_BF16), preferred_element_type=_F32)
                if jb == 0:
                    acc_new_ref[c] = pv
                else:
                    acc_new_ref[c] += pv
        for c in range(2):
            m_new_ref[c] = col_max[c]
            l_new_ref[c] = col_sum[c]

    even = (s_even_ref, m_even_ref, acc_even_ref, l_even_ref)
    odd = (s_odd_ref, m_odd_ref, acc_odd_ref, l_odd_ref)

    @pl.when(step % 2 == 0)
    def _():
        run_step(*even, *odd)

    @pl.when(step % 2 == 1)
    def _():
        run_step(*odd, *even)


def _attention(q_t, kv, slopes, lam_params, gain_col, *, batch, seq, tq, tk, lambda_init):
    _, n_heads, T, _ = kv.shape
    n_q = seq // tq
    n_kb = seq // tk
    n_tiles = batch * n_heads * n_q
    assert tk % tq == 0 and seq <= POS_RADIX * 256
    kern = functools.partial(_attn_kernel, tq=tq, tk=tk, n_kb=n_kb, n_q=n_q, n_heads=n_heads,
                             n_tiles=n_tiles, lambda_init=lambda_init)

    def score_tile(s):
        t = jnp.minimum(s, n_tiles - 1)
        head, i = _divmod(t, n_q)
        b, h = _divmod(head, n_heads)
        return b, h, i

    def output_tile(s):
        return score_tile(jnp.maximum(s - 2, 0))

    def q_index(s):
        b, h, i = score_tile(s)
        return h, 0, b * n_q + i

    def kv_index(which):
        def index(s):
            b, h, _ = score_tile(s)
            return which, h, b, 0
        return index

    def out_index(s):
        b, h, i = output_tile(s)
        return b * n_q + i, h

    scores = pltpu.VMEM((2, n_kb, tk, tq), _F32)
    col_stat = pltpu.VMEM((2, F32_SUBLANES, tq), _F32)
    acc = pltpu.VMEM((2, HEAD_WIDTH, tq), _F32)
    return pl.pallas_call(
        kern,
        grid=(n_tiles + 2,),
        in_specs=[
            pl.BlockSpec(memory_space=pltpu.SMEM),
            pl.BlockSpec((4, HEAD_DIM), lambda s: (0, 0)),
            pl.BlockSpec((HEAD_WIDTH, 1), lambda s: (0, 0)),
            pl.BlockSpec((1, HEAD_WIDTH, tq), q_index),
            pl.BlockSpec((1, 1, seq, HEAD_WIDTH), kv_index(0)),
            pl.BlockSpec((1, 1, seq, HEAD_WIDTH), kv_index(1)),
        ],
        out_specs=pl.BlockSpec((tq, HEAD_WIDTH), out_index),
        out_shape=jax.ShapeDtypeStruct((T, n_heads * HEAD_WIDTH), _BF16),
        scratch_shapes=[
            pltpu.VMEM((2, n_kb, tk, HEAD_WIDTH), _BF16),
            pltpu.VMEM((2, n_kb, HEAD_WIDTH, tk), _BF16),
            pltpu.VMEM((tk // tq, tk, tq), _F32),
            scores, scores, col_stat, col_stat,
            acc, acc, col_stat, col_stat,
        ],
        compiler_params=_params("arbitrary"),
        name="diff_attn",
    )(slopes, lam_params, gain_col, q_t, kv, kv)


def _merge_kernel(o_ref, pz_ref, pzp_ref, pzn_ref, gz_ref, x_ref, bg_ref, pw_ref, ps_ref,
                  wa_ref, wp_ref, wo_ref, g_ref, out_ref, ext_ref, yp_ref, *, tm, seq):
    D = x_ref.shape[1]
    gdim = pw_ref.shape[1]
    rows = tm + 2 * HALO
    t0 = (pl.program_id(0) * tm) % seq
    prev = pzp_ref[...].astype(_F32)
    nxt = pzn_ref[...].astype(_F32)
    ext_ref[0:HALO] = jnp.where(t0 == 0, jnp.zeros_like(prev), prev)
    ext_ref[HALO:HALO + tm] = pz_ref[...].astype(_F32)
    ext_ref[HALO + tm:rows] = jnp.where(t0 + tm == seq, jnp.zeros_like(nxt), nxt)

    def ahead(a, k):
        return pltpu.roll(a, rows - k, axis=0)

    def behind(a, k):
        return pltpu.roll(a, k, axis=0)

    tpos = t0 + lax.broadcasted_iota(jnp.int32, (tm, 1), 0)
    for g, w in enumerate(POOL_WINDOWS):
        half = w // 2
        cols = slice(g * gdim, (g + 1) * gdim)
        e = ext_ref[:, cols]
        run, n = e, 1
        while n < half:
            run = run + ahead(run, n)
            n *= 2
        if half % F32_SUBLANES == 0:
            win = run[HALO - half:HALO - half + tm] + run[HALO:HALO + tm]
        else:
            win = (behind(run, half) + run)[HALO:HALO + tm]
        cnt = (jnp.minimum(tpos + half, seq) - jnp.maximum(tpos - half, 0)).astype(_F32)
        mixed = win * (1.0 / cnt) - e[HALO:HALO + tm]
        yg = jnp.dot(mixed.astype(_BF16), pw_ref[g], preferred_element_type=_F32)
        yp_ref[:, cols] = (yg * ps_ref[:, cols]).astype(_BF16)

    y_attn = jnp.dot(o_ref[...], wa_ref[...], preferred_element_type=_F32)
    y_pool = jnp.dot(yp_ref[...], wp_ref[...], preferred_element_type=_F32)
    gates = jax.nn.sigmoid(gz_ref[...].astype(_F32) + bg_ref[...])
    merged = gates[:, :D] * y_attn + gates[:, D:] * y_pool
    mo = jnp.dot(merged.astype(_BF16), wo_ref[...], preferred_element_type=_F32)
    out_ref[...] = x_ref[...] + _rms(mo, g_ref[...], NORM_EPS)


def _const_spec(shape):
    nd = len(shape)
    return pl.BlockSpec(shape, lambda i: (0,) * nd, pipeline_mode=pl.Buffered(1))


def _merge(o, pz, gz, x2, b_gate, pool_w, pool_scale, w_attn, w_pool, w_out, g_post, *, seq, tm):
    T, D = x2.shape
    aw = o.shape[1]
    hb = tm // HALO
    n_hb = T // HALO
    return pl.pallas_call(
        functools.partial(_merge_kernel, tm=tm, seq=seq),
        grid=(T // tm,),
        in_specs=[
            pl.BlockSpec((tm, aw), lambda i: (i, 0)),
            pl.BlockSpec((tm, aw), lambda i: (i, 0)),
            pl.BlockSpec((HALO, aw), lambda i: (jnp.maximum(i * hb - 1, 0), 0)),
            pl.BlockSpec((HALO, aw), lambda i: (jnp.minimum((i + 1) * hb, n_hb - 1), 0)),
            pl.BlockSpec((tm, 2 * D), lambda i: (i, 0)),
            pl.BlockSpec((tm, D), lambda i: (i, 0)),
            _const_spec(b_gate.shape),
            _const_spec(pool_w.shape),
            _const_spec(pool_scale.shape),
            _const_spec(w_attn.shape),
            _const_spec(w_pool.shape),
            _const_spec(w_out.shape),
            _const_spec(g_post.shape),
        ],
        out_specs=pl.BlockSpec((tm, D), lambda i: (i, 0)),
        out_shape=jax.ShapeDtypeStruct((T, D), _F32),
        scratch_shapes=[
            pltpu.VMEM((tm + 2 * HALO, aw), _F32),
            pltpu.VMEM((tm, aw), _BF16),
        ],
        compiler_params=_params("arbitrary"),
        name="merge",
    )(o, pz, pz, pz, gz, x2, b_gate, pool_w, pool_scale, w_attn, w_pool, w_out, g_post)


def _ffn_kernel(x_ref, xp_ref, xn_ref, g_ref, wg_ref, wv_ref, cwg_ref, cwv_ref, cbg_ref, cbv_ref,
                wd_ref, gp_ref, out_ref, h_ref, acc_ref, *, tm, seq):
    c = pl.program_id(1)
    rows = tm + 2 * HALO

    @pl.when(c == 0)
    def _():
        t0 = (pl.program_id(0) * tm) % seq
        g = g_ref[...]
        hp = _rms(xp_ref[...], g, NORM_EPS)
        hn = _rms(xn_ref[...], g, NORM_EPS)
        h_ref[0:HALO] = jnp.where(t0 == 0, jnp.zeros_like(hp), hp).astype(_BF16)
        h_ref[HALO:HALO + tm] = _rms(x_ref[...], g, NORM_EPS).astype(_BF16)
        h_ref[HALO + tm:rows] = jnp.where(t0 + tm == seq, jnp.zeros_like(hn), hn).astype(_BF16)
        acc_ref[...] = jnp.zeros_like(acc_ref)

    h = h_ref[...]

    def conv(w_ref, cw_ref, cb_ref):
        u = jnp.dot(h, w_ref[...], preferred_element_type=_F32)
        cw = cw_ref[...]
        u_prev = pltpu.roll(u, 1, axis=0)[HALO:HALO + tm]
        u_next = pltpu.roll(u, rows - 1, axis=0)[HALO:HALO + tm]
        return u_prev * cw[0:1] + u[HALO:HALO + tm] * cw[1:2] + u_next * cw[2:3] + cb_ref[...]

    gate = conv(wg_ref, cwg_ref, cbg_ref)
    val = conv(wv_ref, cwv_ref, cbv_ref)
    act = (jax.nn.gelu(gate) * val).astype(_BF16)
    acc_ref[...] += jnp.dot(act, wd_ref[...], preferred_element_type=_F32)

    @pl.when(c == pl.num_programs(1) - 1)
    def _():
        out_ref[...] = x_ref[...] + _rms(acc_ref[...], gp_ref[...], NORM_EPS)


def _ffn(x1, g_pre, w_up, conv_w, conv_b, w_down, g_post, *, seq, tm, tf):
    T, D = x1.shape
    d_ff = w_down.shape[0]
    n_c = d_ff // tf
    hb = tm // HALO
    n_hb = T // HALO
    return pl.pallas_call(
        functools.partial(_ffn_kernel, tm=tm, seq=seq),
        grid=(T // tm, n_c),
        in_specs=[
            pl.BlockSpec((tm, D), lambda i, c: (i, 0)),
            pl.BlockSpec((HALO, D), lambda i, c: (jnp.maximum(i * hb - 1, 0), 0)),
            pl.BlockSpec((HALO, D), lambda i, c: (jnp.minimum((i + 1) * hb, n_hb - 1), 0)),
            pl.BlockSpec((1, D), lambda i, c: (0, 0)),
            pl.BlockSpec((D, tf), lambda i, c: (0, c)),
            pl.BlockSpec((D, tf), lambda i, c: (0, n_c + c)),
            pl.BlockSpec((CONV_WIDTH, tf), lambda i, c: (0, c)),
            pl.BlockSpec((CONV_WIDTH, tf), lambda i, c: (0, n_c + c)),
            pl.BlockSpec((1, tf), lambda i, c: (0, c)),
            pl.BlockSpec((1, tf), lambda i, c: (0, n_c + c)),
            pl.BlockSpec((tf, D), lambda i, c: (c, 0)),
            pl.BlockSpec((1, D), lambda i, c: (0, 0)),
        ],
        out_specs=pl.BlockSpec((tm, D), lambda i, c: (i, 0)),
        out_shape=jax.ShapeDtypeStruct((T, D), _F32),
        scratch_shapes=[
            pltpu.VMEM((tm + 2 * HALO, D), _BF16),
            pltpu.VMEM((tm, D), _F32),
        ],
        compiler_params=_params("arbitrary", "arbitrary"),
        name="conv_ffn",
    )(x1, x1, x1, g_pre, w_up, w_up, conv_w, conv_w, conv_b, conv_b, w_down, g_post)


def _pick(n, pref):
    t = min(n, pref)
    while n % t:
        t //= 2
    return t


def kernel(x, g_pre_mix, w_in, b_gate, lambda_q1, lambda_k1, lambda_q2, lambda_k2, subln_gain,
           w_attn_branch, pool_w, pool_scale, w_pool_branch, w_out, g_post_mix, g_pre_ffn, w_up,
           conv_w, conv_b, w_down, g_post_ffn):
    B, S, D = x.shape
    depth = w_in.shape[0]
    aw = w_attn_branch.shape[1]
    n_heads = aw // HEAD_WIDTH
    d_ff = w_down.shape[1]
    T = B * S
    slopes_np = np.array([2.0 ** (-8.0 * (h + 1) / n_heads) for h in range(n_heads)], dtype=np.float32)
    assert np.all(np.frexp(slopes_np)[0] == 0.5)
    slopes = jnp.asarray(slopes_np)
    row = lambda a: a.reshape(1, -1).astype(_F32)

    x2 = x.reshape(T, D)
    for l in range(depth):
        lambda_init = 0.8 - 0.6 * math.exp(-0.3 * l)
        q_t, kv, pz, gz = _in_proj(x2, row(g_pre_mix[l]), w_in[l].astype(_BF16),
                                   n_heads=n_heads, tm=_pick(T, 1024))
        lam_params = jnp.stack([lambda_q1[l], lambda_k1[l], lambda_q2[l], lambda_k2[l]]).astype(_F32)
        o = _attention(q_t, kv, slopes, lam_params, subln_gain[l].reshape(-1, 1).astype(_F32),
                       batch=B, seq=S, tq=_pick(S, 256), tk=_pick(S, 512), lambda_init=lambda_init)
        x2 = _merge(o, pz, gz, x2, row(b_gate[l]), pool_w[l].astype(_BF16), row(pool_scale[l]),
                    w_attn_branch[l].astype(_BF16), w_pool_branch[l].astype(_BF16),
                    w_out[l].astype(_BF16), row(g_post_mix[l]), seq=S, tm=_pick(S, 256))
        x2 = _ffn(x2, row(g_pre_ffn[l]), w_up[l].astype(_BF16), conv_w[l].astype(_F32),
                  row(conv_b[l]), w_down[l].astype(_BF16), row(g_post_ffn[l]),
                  seq=S, tm=_pick(S, 512), tf=_pick(d_ff, 512))
    return x2.reshape(B, S, D)
```

```python
import functools
import math

import jax
import jax.numpy as jnp
import numpy as np
from jax import lax
from jax.experimental import pallas as pl
from jax.experimental.pallas import tpu as pltpu

HEAD_DIM = 64
HEAD_WIDTH = 2 * HEAD_DIM
POOL_WINDOWS = (2, 4, 8, 16)
CONV_WIDTH = 3
NORM_EPS = 1e-6
SUBLN_EPS = 1e-5
HALO = 16
F32_SUBLANES = 8
MXU_COLS = 256
POS_RADIX = 64
LOG2E = math.log2(math.e)
V7X_VMEM_LIMIT_BYTES = 56 * 1024 * 1024

_F32 = jnp.float32
_BF16 = jnp.bfloat16


def _rms(x, g, eps):
    return x * lax.rsqrt(jnp.mean(x * x, axis=-1, keepdims=True) + eps) * g


def _rms_rows(src, g, eps, store):
    for r0 in range(0, src.shape[0], F32_SUBLANES):
        rows = slice(r0, r0 + F32_SUBLANES)
        store(rows, _rms(src[rows], g, eps))


def _divmod(x, n):
    if n & (n - 1) == 0:
        return x >> (n.bit_length() - 1), x & (n - 1)
    return x // n, x % n


def _params(*sem):
    return pltpu.CompilerParams(dimension_semantics=sem, vmem_limit_bytes=V7X_VMEM_LIMIT_BYTES)


def _in_proj_kernel(x_ref, g_ref, w_ref, qt_ref, kv_ref, pz_ref, gz_ref, h_ref, *, n_heads):
    j = pl.program_id(1)

    @pl.when(j == 0)
    def _():
        h_ref[...] = _rms(x_ref[...], g_ref[...], NORM_EPS).astype(_BF16)

    def project(store):
        h = h_ref[...]
        for lo in range(0, w_ref.shape[1], MXU_COLS):
            store(lo, jnp.dot(h, w_ref[:, lo:lo + MXU_COLS], preferred_element_type=_F32))

    def heads_of(lo, z):
        for k in range(MXU_COLS // HEAD_WIDTH):
            yield lo // HEAD_WIDTH + k, z[:, k * HEAD_WIDTH:(k + 1) * HEAD_WIDTH]

    def store_q(lo, z):
        for head, z_h in heads_of(lo, z):
            qt_ref[head] = (z_h * (HEAD_DIM ** -0.5)).T.astype(_BF16)

    def store_kv(lo, z):
        for head, z_h in heads_of(lo, z):
            kv_ref[0, head] = z_h.astype(_BF16)

    def store_pz(lo, z):
        pz_ref[:, lo:lo + MXU_COLS] = z.astype(_BF16)

    def store_gz(lo, z):
        gz_ref[:, lo:lo + MXU_COLS] = z.astype(_BF16)

    @pl.when(j == 0)
    def _():
        project(store_q)

    @pl.when((j == 1) | (j == 2))
    def _():
        project(store_kv)

    @pl.when(j == 3)
    def _():
        project(store_pz)

    @pl.when(j > 3)
    def _():
        project(store_gz)


def _in_proj(x2, g, w, *, n_heads, tm):
    T, D = x2.shape
    aw = n_heads * HEAD_WIDTH
    n_gate_blocks = 2 * D // aw
    grid = (T // tm, 4 + n_gate_blocks)
    return pl.pallas_call(
        functools.partial(_in_proj_kernel, n_heads=n_heads),
        grid=grid,
        in_specs=[
            pl.BlockSpec((tm, D), lambda i, j: (i, 0)),
            pl.BlockSpec((1, D), lambda i, j: (0, 0)),
            pl.BlockSpec((D, aw), lambda i, j: (0, j)),
        ],
        out_specs=[
            pl.BlockSpec((n_heads, HEAD_WIDTH, tm), lambda i, j: (0, 0, i)),
            pl.BlockSpec((1, n_heads, tm, HEAD_WIDTH), lambda i, j: (jnp.clip(j - 1, 0, 1), 0, i, 0)),
            pl.BlockSpec((tm, aw), lambda i, j: (i, 0)),
            pl.BlockSpec((tm, aw), lambda i, j: (i, jnp.maximum(j - 4, 0))),
        ],
        out_shape=[
            jax.ShapeDtypeStruct((n_heads, HEAD_WIDTH, T), _BF16),
            jax.ShapeDtypeStruct((2, n_heads, T, HEAD_WIDTH), _BF16),
            jax.ShapeDtypeStruct((T, aw), _BF16),
            jax.ShapeDtypeStruct((T, 2 * D), _BF16),
        ],
        scratch_shapes=[pltpu.VMEM((tm, D), _BF16)],
        compiler_params=_params("arbitrary", "arbitrary"),
        name="in_proj",
    )(x2, g, w)


def _pos_features(pos, idx, base, values):
    hi = (pos // POS_RADIX).astype(_F32)
    lo = (pos % POS_RADIX).astype(_F32)
    out = jnp.zeros(pos.shape, _F32)
    for n, val in enumerate(values(hi, lo)):
        out = jnp.where(idx == base + n, val, out)
    return out


def _attn_kernel(slopes_ref, lam_ref, gain_ref, q_ref, k_ref, v_ref, o_ref,
                 kaug_ref, vt_ref, corr_ref, s_even_ref, s_odd_ref, m_even_ref, m_odd_ref,
                 acc_even_ref, acc_odd_ref, l_even_ref, l_odd_ref,
                 *, tq, tk, n_kb, n_q, n_heads, n_tiles, lambda_init):
    step = pl.program_id(0)
    tile = jnp.minimum(step, n_tiles - 1)
    head, qi = _divmod(tile, n_q)
    prev_head, _ = _divmod(jnp.minimum(jnp.maximum(step - 1, 0), n_tiles - 1), n_q)
    feat_base = (HEAD_DIM, 0)

    @pl.when(step == 0)
    def _():
        s_odd_ref[...] = jnp.zeros_like(s_odd_ref)
        m_odd_ref[...] = jnp.zeros_like(m_odd_ref)
        acc_odd_ref[...] = jnp.zeros_like(acc_odd_ref)
        l_odd_ref[...] = jnp.ones_like(l_odd_ref)

    @pl.when((qi == 0) & (step < n_tiles))
    def _():
        slope = slopes_ref[_divmod(head, n_heads)[1]]
        lane = lax.broadcasted_iota(jnp.int32, (tk, HEAD_WIDTH), 1)
        row = lax.broadcasted_iota(jnp.int32, (tk, HEAD_WIDTH), 0)
        feats, feats_per_block, own_half = [], [], []
        for c in range(2):
            feats.append(_pos_features(
                row, lane, feat_base[c],
                lambda hi, lo: (-POS_RADIX * slope, -slope, POS_RADIX * slope * hi, slope * lo)))
            feats_per_block.append(jnp.where(lane == feat_base[c] + 2, slope * tk, 0.0))
            own_half.append((lane >= c * HEAD_DIM) & (lane < (c + 1) * HEAD_DIM))
        for jb in range(n_kb):
            kb = k_ref[0, 0, jb * tk:(jb + 1) * tk, :].astype(_F32)
            for c in range(2):
                kaug_ref[c, jb] = jnp.where(own_half[c], kb, feats[c]).astype(_BF16)
                feats[c] = feats[c] + feats_per_block[c]
            vb = v_ref[0, 0, jb * tk:(jb + 1) * tk, :].astype(_F32)
            vt_ref[head % 2, jb] = vb.T.astype(_BF16)
        rel = (lax.broadcasted_iota(jnp.int32, (tk, tq), 0)
               - lax.broadcasted_iota(jnp.int32, (tk, tq), 1))
        for v in range(tk // tq):
            corr_ref[v] = (2.0 * slope) * jnp.maximum(rel - v * tq, 0).astype(_F32)

    q_t = q_ref[0]
    q_half = (q_t[:HEAD_DIM], q_t[HEAD_DIM:])
    sub = lax.broadcasted_iota(jnp.int32, (HALO, tq), 0)
    qfeat = _pos_features(qi * tq + lax.broadcasted_iota(jnp.int32, (HALO, tq), 1),
                          sub, 0, lambda hi, lo: (hi, lo, 1.0, 1.0)).astype(_BF16)
    no_feat = jnp.zeros((HEAD_DIM - HALO, tq), _BF16)

    def with_features(c, f):
        other_half = jnp.concatenate([f, no_feat], axis=0)
        parts = (q_half[0], other_half) if c == 0 else (other_half, q_half[1])
        return jnp.concatenate(parts, axis=0)

    q_left = [with_features(c, qfeat) for c in range(2)]
    q_right = [with_features(c, -qfeat) for c in range(2)]

    jd, offset_in_block = _divmod(qi * tq, tk)
    correction = corr_ref[_divmod(offset_in_block, tq)[0]]

    lam_p = lam_ref[...]
    lam = (jnp.exp(jnp.sum(lam_p[0:1] * lam_p[1:2], axis=-1, keepdims=True))
           - jnp.exp(jnp.sum(lam_p[2:3] * lam_p[3:4], axis=-1, keepdims=True))
           + lambda_init)

    def run_step(s_new_ref, m_new_ref, acc_new_ref, l_new_ref,
                 s_old_ref, m_old_ref, acc_old_ref, l_old_ref):
        l_old = [jnp.sum(l_old_ref[c], axis=0, keepdims=True) for c in range(2)]
        o_t = acc_old_ref[0] / l_old[0] - lam * (acc_old_ref[1] / l_old[1])
        ms = jnp.mean(o_t * o_t, axis=0, keepdims=True)
        o_t = o_t * lax.rsqrt(ms + SUBLN_EPS) * gain_ref[...] * (1.0 - lambda_init)
        o_ref[...] = o_t.T.astype(o_ref.dtype)

        def score_block(d, col_max):
            jb = jd if d == 0 else _divmod(jd + d, n_kb)[1]
            out = []
            for c in range(2):
                q_c = q_left[c] if d == 0 else jnp.where(jb < jd, q_left[c], q_right[c])
                s = jnp.dot(kaug_ref[c, jb], q_c, preferred_element_type=_F32)
                if d == 0:
                    s = s - correction
                s = s * LOG2E
                s_new_ref[c, jb] = s
                block_max = jnp.max(s.reshape(tk // 8, 8, tq), axis=0)
                out.append(block_max if col_max is None else jnp.maximum(col_max[c], block_max))
            return out

        m_old = [jnp.max(m_old_ref[c], axis=0, keepdims=True) for c in range(2)]
        col_sum = [jnp.zeros((8, tq), _F32), jnp.zeros((8, tq), _F32)]
        col_max = None
        for jb in range(n_kb):
            col_max = score_block(jb, col_max)
            v_t = vt_ref[prev_head % 2, jb]
            for c in range(2):
                p = jnp.exp2(s_old_ref[c, jb] - m_old[c])
                col_sum[c] = col_sum[c] + jnp.sum(p.reshape(tk // 8, 8, tq), axis=0)
                pv = jnp.dot(v_t, p.astype(_BF16), preferred_element_type=_F32)
                if jb == 0:
                    acc_new_ref[c] = pv
                else:
                    acc_new_ref[c] += pv
        for c in range(2):
            m_new_ref[c] = col_max[c]
            l_new_ref[c] = col_sum[c]

    even = (s_even_ref, m_even_ref, acc_even_ref, l_even_ref)
    odd = (s_odd_ref, m_odd_ref, acc_odd_ref, l_odd_ref)

    @pl.when(step % 2 == 0)
    def _():
        run_step(*even, *odd)

    @pl.when(step % 2 == 1)
    def _():
        run_step(*odd, *even)


def _attention(q_t, kv, slopes, lam_params, gain_col, *, batch, seq, tq, tk, lambda_init):
    _, n_heads, T, _ = kv.shape
    n_q = seq // tq
    n_kb = seq // tk
    n_tiles = batch * n_heads * n_q
    assert tk % tq == 0 and seq <= POS_RADIX * 256
    kern = functools.partial(_attn_kernel, tq=tq, tk=tk, n_kb=n_kb, n_q=n_q, n_heads=n_heads,
                             n_tiles=n_tiles, lambda_init=lambda_init)

    def score_tile(s):
        t = jnp.minimum(s, n_tiles - 1)
        head, i = _divmod(t, n_q)
        b, h = _divmod(head, n_heads)
        return b, h, i

    def output_tile(s):
        return score_tile(jnp.maximum(s - 2, 0))

    def q_index(s):
        b, h, i = score_tile(s)
        return h, 0, b * n_q + i

    def kv_index(which):
        def index(s):
            b, h, _ = score_tile(s)
            return which, h, b, 0
        return index

    def out_index(s):
        b, h, i = output_tile(s)
        return b * n_q + i, h

    scores = pltpu.VMEM((2, n_kb, tk, tq), _F32)
    col_stat = pltpu.VMEM((2, F32_SUBLANES, tq), _F32)
    acc = pltpu.VMEM((2, HEAD_WIDTH, tq), _F32)
    return pl.pallas_call(
        kern,
        grid=(n_tiles + 2,),
        in_specs=[
            pl.BlockSpec(memory_space=pltpu.SMEM),
            pl.BlockSpec((4, HEAD_DIM), lambda s: (0, 0)),
            pl.BlockSpec((HEAD_WIDTH, 1), lambda s: (0, 0)),
            pl.BlockSpec((1, HEAD_WIDTH, tq), q_index),
            pl.BlockSpec((1, 1, seq, HEAD_WIDTH), kv_index(0)),
            pl.BlockSpec((1, 1, seq, HEAD_WIDTH), kv_index(1)),
        ],
        out_specs=pl.BlockSpec((tq, HEAD_WIDTH), out_index),
        out_shape=jax.ShapeDtypeStruct((T, n_heads * HEAD_WIDTH), _BF16),
        scratch_shapes=[
            pltpu.VMEM((2, n_kb, tk, HEAD_WIDTH), _BF16),
            pltpu.VMEM((2, n_kb, HEAD_WIDTH, tk), _BF16),
            pltpu.VMEM((tk // tq, tk, tq), _F32),
            scores, scores, col_stat, col_stat,
            acc, acc, col_stat, col_stat,
        ],
        compiler_params=_params("arbitrary"),
        name="diff_attn",
    )(slopes, lam_params, gain_col, q_t, kv, kv)


def _merge_kernel(o_ref, pz_ref, pzp_ref, pzn_ref, gz_ref, x_ref, bg_ref, pw_ref, ps_ref,
                  wa_ref, wp_ref, wo_ref, g_ref, out_ref, ext_ref, yp_ref, *, tm, seq):
    D = x_ref.shape[1]
    gdim = pw_ref.shape[1]
    rows = tm + 2 * HALO
    t0 = (pl.program_id(0) * tm) % seq
    prev = pzp_ref[...].astype(_F32)
    nxt = pzn_ref[...].astype(_F32)
    ext_ref[0:HALO] = jnp.where(t0 == 0, jnp.zeros_like(prev), prev)
    ext_ref[HALO:HALO + tm] = pz_ref[...].astype(_F32)
    ext_ref[HALO + tm:rows] = jnp.where(t0 + tm == seq, jnp.zeros_like(nxt), nxt)

    def ahead(a, k):
        return pltpu.roll(a, rows - k, axis=0)

    def behind(a, k):
        return pltpu.roll(a, k, axis=0)

    tpos = t0 + lax.broadcasted_iota(jnp.int32, (tm, 1), 0)
    for g, w in enumerate(POOL_WINDOWS):
        half = w // 2
        cols = slice(g * gdim, (g + 1) * gdim)
        e = ext_ref[:, cols]
        run, n = e, 1
        while n < half:
            run = run + ahead(run, n)
            n *= 2
        if half % F32_SUBLANES == 0:
            win = run[HALO - half:HALO - half + tm] + run[HALO:HALO + tm]
        else:
            win = (behind(run, half) + run)[HALO:HALO + tm]
        cnt = (jnp.minimum(tpos + half, seq) - jnp.maximum(tpos - half, 0)).astype(_F32)
        mixed = win * (1.0 / cnt) - e[HALO:HALO + tm]
        yg = jnp.dot(mixed.astype(_BF16), pw_ref[g], preferred_element_type=_F32)
        yp_ref[:, cols] = (yg * ps_ref[:, cols]).astype(_BF16)

    y_attn = jnp.dot(o_ref[...], wa_ref[...], preferred_element_type=_F32)
    y_pool = jnp.dot(yp_ref[...], wp_ref[...], preferred_element_type=_F32)
    gates = jax.nn.sigmoid(gz_ref[...].astype(_F32) + bg_ref[...])
    merged = gates[:, :D] * y_attn + gates[:, D:] * y_pool
    mo = jnp.dot(merged.astype(_BF16), wo_ref[...], preferred_element_type=_F32)
    out_ref[...] = x_ref[...] + _rms(mo, g_ref[...], NORM_EPS)


def _const_spec(shape):
    nd = len(shape)
    return pl.BlockSpec(shape, lambda i: (0,) * nd, pipeline_mode=pl.Buffered(1))


def _merge(o, pz, gz, x2, b_gate, pool_w, pool_scale, w_attn, w_pool, w_out, g_post, *, seq, tm):
    T, D = x2.shape
    aw = o.shape[1]
    hb = tm // HALO
    n_hb = T // HALO
    return pl.pallas_call(
        functools.partial(_merge_kernel, tm=tm, seq=seq),
        grid=(T // tm,),
        in_specs=[
            pl.BlockSpec((tm, aw), lambda i: (i, 0)),
            pl.BlockSpec((tm, aw), lambda i: (i, 0)),
            pl.BlockSpec((HALO, aw), lambda i: (jnp.maximum(i * hb - 1, 0), 0)),
            pl.BlockSpec((HALO, aw), lambda i: (jnp.minimum((i + 1) * hb, n_hb - 1), 0)),
            pl.BlockSpec((tm, 2 * D), lambda i: (i, 0)),
            pl.BlockSpec((tm, D), lambda i: (i, 0)),
            _const_spec(b_gate.shape),
            _const_spec(pool_w.shape),
            _const_spec(pool_scale.shape),
            _const_spec(w_attn.shape),
            _const_spec(w_pool.shape),
            _const_spec(w_out.shape),
            _const_spec(g_post.shape),
        ],
        out_specs=pl.BlockSpec((tm, D), lambda i: (i, 0)),
        out_shape=jax.ShapeDtypeStruct((T, D), _F32),
        scratch_shapes=[
            pltpu.VMEM((tm + 2 * HALO, aw), _F32),
            pltpu.VMEM((tm, aw), _BF16),
        ],
        compiler_params=_params("arbitrary"),
        name="merge",
    )(o, pz, pz, pz, gz, x2, b_gate, pool_w, pool_scale, w_attn, w_pool, w_out, g_post)


def _ffn_kernel(x_ref, xp_ref, xn_ref, g_ref, w_ref, cwg_ref, cwv_ref, cbg_ref, cbv_ref,
                wd_ref, gp_ref, out_ref, h_ref, acc_ref, *, tm, seq):
    c = pl.program_id(1)
    rows = tm + 2 * HALO

    @pl.when(c == 0)
    def _():
        t0 = (pl.program_id(0) * tm) % seq
        g = g_ref[...]
        hp = _rms(xp_ref[...], g, NORM_EPS)
        hn = _rms(xn_ref[...], g, NORM_EPS)
        h_ref[0:HALO] = jnp.where(t0 == 0, jnp.zeros_like(hp), hp).astype(_BF16)
        h_ref[HALO:HALO + tm] = _rms(x_ref[...], g, NORM_EPS).astype(_BF16)
        h_ref[HALO + tm:rows] = jnp.where(t0 + tm == seq, jnp.zeros_like(hn), hn).astype(_BF16)
        acc_ref[...] = jnp.zeros_like(acc_ref)

    h = h_ref[...]

    tf = wd_ref.shape[0]
    u_both = jnp.dot(h, w_ref[0], preferred_element_type=_F32)

    def conv(u, cw_ref, cb_ref):
        cw = cw_ref[...]
        u_prev = pltpu.roll(u, 1, axis=0)[HALO:HALO + tm]
        u_next = pltpu.roll(u, rows - 1, axis=0)[HALO:HALO + tm]
        return u_prev * cw[0:1] + u[HALO:HALO + tm] * cw[1:2] + u_next * cw[2:3] + cb_ref[...]

    gate = conv(u_both[:, :tf], cwg_ref, cbg_ref)
    val = conv(u_both[:, tf:], cwv_ref, cbv_ref)
    act = (jax.nn.gelu(gate) * val).astype(_BF16)
    acc_ref[...] += jnp.dot(act, wd_ref[...], preferred_element_type=_F32)

    @pl.when(c == pl.num_programs(1) - 1)
    def _():
        def store(rows, y):
            out_ref[rows] = x_ref[rows] + y

        _rms_rows(acc_ref, gp_ref[...], NORM_EPS, store)


def _ffn(x1, g_pre, w_up, conv_w, conv_b, w_down, g_post, *, seq, tm, tf):
    T, D = x1.shape
    d_ff = w_down.shape[0]
    n_c = d_ff // tf
    hb = tm // HALO
    n_hb = T // HALO
    w_up_chunks = w_up.reshape(D, 2, n_c, tf).transpose(2, 0, 1, 3).reshape(n_c, D, 2 * tf)
    return pl.pallas_call(
        functools.partial(_ffn_kernel, tm=tm, seq=seq),
        grid=(T // tm, n_c),
        in_specs=[
            pl.BlockSpec((tm, D), lambda i, c: (i, 0)),
            pl.BlockSpec((HALO, D), lambda i, c: (jnp.maximum(i * hb - 1, 0), 0)),
            pl.BlockSpec((HALO, D), lambda i, c: (jnp.minimum((i + 1) * hb, n_hb - 1), 0)),
            pl.BlockSpec((1, D), lambda i, c: (0, 0)),
            pl.BlockSpec((1, D, 2 * tf), lambda i, c: (c, 0, 0)),
            pl.BlockSpec((CONV_WIDTH, tf), lambda i, c: (0, c)),
            pl.BlockSpec((CONV_WIDTH, tf), lambda i, c: (0, n_c + c)),
            pl.BlockSpec((1, tf), lambda i, c: (0, c)),
            pl.BlockSpec((1, tf), lambda i, c: (0, n_c + c)),
            pl.BlockSpec((tf, D), lambda i, c: (c, 0)),
            pl.BlockSpec((1, D), lambda i, c: (0, 0)),
        ],
        out_specs=pl.BlockSpec((tm, D), lambda i, c: (i, 0)),
        out_shape=jax.ShapeDtypeStruct((T, D), _F32),
        scratch_shapes=[
            pltpu.VMEM((tm + 2 * HALO, D), _BF16),
            pltpu.VMEM((tm, D), _F32),
        ],
        compiler_params=_params("arbitrary", "arbitrary"),
        name="conv_ffn",
    )(x1, x1, x1, g_pre, w_up_chunks, conv_w, conv_w, conv_b, conv_b, w_down, g_post)


def _pick(n, pref):
    t = min(n, pref)
    while n % t:
        t //= 2
    return t


def kernel(x, g_pre_mix, w_in, b_gate, lambda_q1, lambda_k1, lambda_q2, lambda_k2, subln_gain,
           w_attn_branch, pool_w, pool_scale, w_pool_branch, w_out, g_post_mix, g_pre_ffn, w_up,
           conv_w, conv_b, w_down, g_post_ffn):
    B, S, D = x.shape
    depth = w_in.shape[0]
    aw = w_attn_branch.shape[1]
    n_heads = aw // HEAD_WIDTH
    d_ff = w_down.shape[1]
    T = B * S
    slopes_np = np.array([2.0 ** (-8.0 * (h + 1) / n_heads) for h in range(n_heads)], dtype=np.float32)
    assert np.all(np.frexp(slopes_np)[0] == 0.5)
    slopes = jnp.asarray(slopes_np)
    row = lambda a: a.reshape(1, -1).astype(_F32)

    x2 = x.reshape(T, D)
    for l in range(depth):
        lambda_init = 0.8 - 0.6 * math.exp(-0.3 * l)
        q_t, kv, pz, gz = _in_proj(x2, row(g_pre_mix[l]), w_in[l].astype(_BF16),
                                   n_heads=n_heads, tm=_pick(T, 1024))
        lam_params = jnp.stack([lambda_q1[l], lambda_k1[l], lambda_q2[l], lambda_k2[l]]).astype(_F32)
        o = _attention(q_t, kv, slopes, lam_params, subln_gain[l].reshape(-1, 1).astype(_F32),
                       batch=B, seq=S, tq=_pick(S, 256), tk=_pick(S, 512), lambda_init=lambda_init)
        x2 = _merge(o, pz, gz, x2, row(b_gate[l]), pool_w[l].astype(_BF16), row(pool_scale[l]),
                    w_attn_branch[l].astype(_BF16), w_pool_branch[l].astype(_BF16),
                    w_out[l].astype(_BF16), row(g_post_mix[l]), seq=S, tm=_pick(S, 256))
        x2 = _ffn(x2, row(g_pre_ffn[l]), w_up[l].astype(_BF16), conv_w[l].astype(_F32),
                  row(conv_b[l]), w_down[l].astype(_BF16), row(g_post_ffn[l]),
                  seq=S, tm=_pick(S, 512), tf=_pick(d_ff, 512))
    return x2.reshape(B, S, D)
```

```python
import functools
import math

import jax
import jax.numpy as jnp
import numpy as np
from jax import lax
from jax.experimental import pallas as pl
from jax.experimental.pallas import tpu as pltpu

HEAD_DIM = 64
HEAD_WIDTH = 2 * HEAD_DIM
POOL_WINDOWS = (2, 4, 8, 16)
CONV_WIDTH = 3
NORM_EPS = 1e-6
SUBLN_EPS = 1e-5
HALO = 16
F32_SUBLANES = 8
MXU_COLS = 256
POS_RADIX = 64
LOG2E = math.log2(math.e)
V7X_VMEM_LIMIT_BYTES = 56 * 1024 * 1024

_F32 = jnp.float32
_BF16 = jnp.bfloat16


def _rms(x, g, eps):
    return x * lax.rsqrt(jnp.mean(x * x, axis=-1, keepdims=True) + eps) * g


def _rms_rows(src, g, eps, store):
    for r0 in range(0, src.shape[0], F32_SUBLANES):
        rows = slice(r0, r0 + F32_SUBLANES)
        store(rows, _rms(src[rows], g, eps))


def _divmod(x, n):
    if n & (n - 1) == 0:
        return x >> (n.bit_length() - 1), x & (n - 1)
    return x // n, x % n


def _params(*sem):
    return pltpu.CompilerParams(dimension_semantics=sem, vmem_limit_bytes=V7X_VMEM_LIMIT_BYTES)


def _in_proj_kernel(x_ref, g_ref, w_ref, qt_ref, kv_ref, pz_ref, gz_ref, h_ref, *, n_heads):
    j = pl.program_id(1)

    @pl.when(j == 0)
    def _():
        h_ref[...] = _rms(x_ref[...], g_ref[...], NORM_EPS).astype(_BF16)

    def project(store):
        h = h_ref[...]
        for lo in range(0, w_ref.shape[1], MXU_COLS):
            store(lo, jnp.dot(h, w_ref[:, lo:lo + MXU_COLS], preferred_element_type=_F32))

    def heads_of(lo, z):
        for k in range(MXU_COLS // HEAD_WIDTH):
            yield lo // HEAD_WIDTH + k, z[:, k * HEAD_WIDTH:(k + 1) * HEAD_WIDTH]

    def store_q(lo, z):
        for head, z_h in heads_of(lo, z):
            qt_ref[head] = (z_h * (HEAD_DIM ** -0.5)).T.astype(_BF16)

    def store_kv(lo, z):
        for head, z_h in heads_of(lo, z):
            kv_ref[0, head] = z_h.astype(_BF16)

    def store_pz(lo, z):
        pz_ref[:, lo:lo + MXU_COLS] = z.astype(_BF16)

    def store_gz(lo, z):
        gz_ref[:, lo:lo + MXU_COLS] = z.astype(_BF16)

    @pl.when(j == 0)
    def _():
        project(store_q)

    @pl.when((j == 1) | (j == 2))
    def _():
        project(store_kv)

    @pl.when(j == 3)
    def _():
        project(store_pz)

    @pl.when(j > 3)
    def _():
        project(store_gz)


def _in_proj(x2, g, w, *, n_heads, tm):
    T, D = x2.shape
    aw = n_heads * HEAD_WIDTH
    n_gate_blocks = 2 * D // aw
    grid = (T // tm, 4 + n_gate_blocks)
    return pl.pallas_call(
        functools.partial(_in_proj_kernel, n_heads=n_heads),
        grid=grid,
        in_specs=[
            pl.BlockSpec((tm, D), lambda i, j: (i, 0)),
            pl.BlockSpec((1, D), lambda i, j: (0, 0)),
            pl.BlockSpec((D, aw), lambda i, j: (0, j)),
        ],
        out_specs=[
            pl.BlockSpec((n_heads, HEAD_WIDTH, tm), lambda i, j: (0, 0, i)),
            pl.BlockSpec((1, n_heads, tm, HEAD_WIDTH), lambda i, j: (jnp.clip(j - 1, 0, 1), 0, i, 0)),
            pl.BlockSpec((tm, aw), lambda i, j: (i, 0)),
            pl.BlockSpec((tm, aw), lambda i, j: (i, jnp.maximum(j - 4, 0))),
        ],
        out_shape=[
            jax.ShapeDtypeStruct((n_heads, HEAD_WIDTH, T), _BF16),
            jax.ShapeDtypeStruct((2, n_heads, T, HEAD_WIDTH), _BF16),
            jax.ShapeDtypeStruct((T, aw), _BF16),
            jax.ShapeDtypeStruct((T, 2 * D), _BF16),
        ],
        scratch_shapes=[pltpu.VMEM((tm, D), _BF16)],
        compiler_params=_params("arbitrary", "arbitrary"),
        name="in_proj",
    )(x2, g, w)


def _pos_features(pos, idx, base, values):
    hi = (pos // POS_RADIX).astype(_F32)
    lo = (pos % POS_RADIX).astype(_F32)
    out = jnp.zeros(pos.shape, _F32)
    for n, val in enumerate(values(hi, lo)):
        out = jnp.where(idx == base + n, val, out)
    return out


def _attn_kernel(slopes_ref, lam_ref, gain_ref, q_ref, k_ref, v_ref, o_ref,
                 kaug_ref, vt_ref, corr_ref, s_even_ref, s_odd_ref, m_even_ref, m_odd_ref,
                 acc_even_ref, acc_odd_ref, l_even_ref, l_odd_ref,
                 *, tq, tk, n_kb, n_q, n_heads, n_tiles, lambda_init):
    step = pl.program_id(0)
    tile = jnp.minimum(step, n_tiles - 1)
    head, qi = _divmod(tile, n_q)
    prev_head, _ = _divmod(jnp.minimum(jnp.maximum(step - 1, 0), n_tiles - 1), n_q)
    feat_base = (HEAD_DIM, 0)

    @pl.when(step == 0)
    def _():
        s_odd_ref[...] = jnp.zeros_like(s_odd_ref)
        m_odd_ref[...] = jnp.zeros_like(m_odd_ref)
        acc_odd_ref[...] = jnp.zeros_like(acc_odd_ref)
        l_odd_ref[...] = jnp.ones_like(l_odd_ref)

    @pl.when((qi == 0) & (step < n_tiles))
    def _():
        slope = slopes_ref[_divmod(head, n_heads)[1]]
        lane = lax.broadcasted_iota(jnp.int32, (tk, HEAD_WIDTH), 1)
        row = lax.broadcasted_iota(jnp.int32, (tk, HEAD_WIDTH), 0)
        feats, feats_per_block, own_half = [], [], []
        for c in range(2):
            feats.append(_pos_features(
                row, lane, feat_base[c],
                lambda hi, lo: (-POS_RADIX * slope, -slope, POS_RADIX * slope * hi, slope * lo)))
            feats_per_block.append(jnp.where(lane == feat_base[c] + 2, slope * tk, 0.0))
            own_half.append((lane >= c * HEAD_DIM) & (lane < (c + 1) * HEAD_DIM))
        for jb in range(n_kb):
            kb = k_ref[0, 0, jb * tk:(jb + 1) * tk, :].astype(_F32)
            for c in range(2):
                kaug_ref[c, jb] = jnp.where(own_half[c], kb, feats[c]).astype(_BF16)
                feats[c] = feats[c] + feats_per_block[c]
            vb = v_ref[0, 0, jb * tk:(jb + 1) * tk, :].astype(_F32)
            vt_ref[head % 2, jb] = vb.T.astype(_BF16)
        rel = (lax.broadcasted_iota(jnp.int32, (tk, tq), 0)
               - lax.broadcasted_iota(jnp.int32, (tk, tq), 1))
        for v in range(tk // tq):
            corr_ref[v] = (2.0 * slope) * jnp.maximum(rel - v * tq, 0).astype(_F32)

    q_t = q_ref[0]
    q_half = (q_t[:HEAD_DIM], q_t[HEAD_DIM:])
    sub = lax.broadcasted_iota(jnp.int32, (HALO, tq), 0)
    qfeat = _pos_features(qi * tq + lax.broadcasted_iota(jnp.int32, (HALO, tq), 1),
                          sub, 0, lambda hi, lo: (hi, lo, 1.0, 1.0)).astype(_BF16)
    no_feat = jnp.zeros((HEAD_DIM - HALO, tq), _BF16)

    def with_features(c, f):
        other_half = jnp.concatenate([f, no_feat], axis=0)
        parts = (q_half[0], other_half) if c == 0 else (other_half, q_half[1])
        return jnp.concatenate(parts, axis=0)

    q_left = [with_features(c, qfeat) for c in range(2)]
    q_right = [with_features(c, -qfeat) for c in range(2)]

    jd, offset_in_block = _divmod(qi * tq, tk)
    correction = corr_ref[_divmod(offset_in_block, tq)[0]]

    lam_p = lam_ref[...]
    lam = (jnp.exp(jnp.sum(lam_p[0:1] * lam_p[1:2], axis=-1, keepdims=True))
           - jnp.exp(jnp.sum(lam_p[2:3] * lam_p[3:4], axis=-1, keepdims=True))
           + lambda_init)

    def run_step(s_new_ref, m_new_ref, acc_new_ref, l_new_ref,
                 s_old_ref, m_old_ref, acc_old_ref, l_old_ref):
        l_old = [jnp.sum(l_old_ref[c], axis=0, keepdims=True) for c in range(2)]
        o_t = acc_old_ref[0] / l_old[0] - lam * (acc_old_ref[1] / l_old[1])
        ms = jnp.mean(o_t * o_t, axis=0, keepdims=True)
        o_t = o_t * lax.rsqrt(ms + SUBLN_EPS) * gain_ref[...] * (1.0 - lambda_init)
        o_ref[...] = o_t.T.astype(o_ref.dtype)

        def score_block(d, col_max):
            jb = jd if d == 0 else _divmod(jd + d, n_kb)[1]
            out = []
            for c in range(2):
                q_c = q_left[c] if d == 0 else jnp.where(jb < jd, q_left[c], q_right[c])
                s = jnp.dot(kaug_ref[c, jb], q_c, preferred_element_type=_F32)
                if d == 0:
                    s = s - correction
                s = s * LOG2E
                s_new_ref[c, jb] = s
                block_max = jnp.max(s.reshape(tk // 8, 8, tq), axis=0)
                out.append(block_max if col_max is None else jnp.maximum(col_max[c], block_max))
            return out

        m_old = [jnp.max(m_old_ref[c], axis=0, keepdims=True) for c in range(2)]
        col_sum = [jnp.zeros((8, tq), _F32), jnp.zeros((8, tq), _F32)]
        col_max = None
        for jb in range(n_kb):
            col_max = score_block(jb, col_max)
            v_t = vt_ref[prev_head % 2, jb]
            for c in range(2):
                p = jnp.exp2(s_old_ref[c, jb] - m_old[c])
                col_sum[c] = col_sum[c] + jnp.sum(p.reshape(tk // 8, 8, tq), axis=0)
                pv = jnp.dot(v_t, p.astype(_BF16), preferred_element_type=_F32)
                if jb == 0:
                    acc_new_ref[c] = pv
                else:
                    acc_new_ref[c] += pv
        for c in range(2):
            m_new_ref[c] = col_max[c]
            l_new_ref[c] = col_sum[c]

    even = (s_even_ref, m_even_ref, acc_even_ref, l_even_ref)
    odd = (s_odd_ref, m_odd_ref, acc_odd_ref, l_odd_ref)

    @pl.when(step % 2 == 0)
    def _():
        run_step(*even, *odd)

    @pl.when(step % 2 == 1)
    def _():
        run_step(*odd, *even)


def _attention(q_t, kv, slopes, lam_params, gain_col, *, batch, seq, tq, tk, lambda_init):
    _, n_heads, T, _ = kv.shape
    n_q = seq // tq
    n_kb = seq // tk
    n_tiles = batch * n_heads * n_q
    assert tk % tq == 0 and seq <= POS_RADIX * 256
    kern = functools.partial(_attn_kernel, tq=tq, tk=tk, n_kb=n_kb, n_q=n_q, n_heads=n_heads,
                             n_tiles=n_tiles, lambda_init=lambda_init)

    def score_tile(s):
        t = jnp.minimum(s, n_tiles - 1)
        head, i = _divmod(t, n_q)
        b, h = _divmod(head, n_heads)
        return b, h, i

    def output_tile(s):
        return score_tile(jnp.maximum(s - 2, 0))

    def q_index(s):
        b, h, i = score_tile(s)
        return h, 0, b * n_q + i

    def kv_index(which):
        def index(s):
            b, h, _ = score_tile(s)
            return which, h, b, 0
        return index

    def out_index(s):
        b, h, i = output_tile(s)
        return b * n_q + i, h

    scores = pltpu.VMEM((2, n_kb, tk, tq), _F32)
    col_stat = pltpu.VMEM((2, F32_SUBLANES, tq), _F32)
    acc = pltpu.VMEM((2, HEAD_WIDTH, tq), _F32)
    return pl.pallas_call(
        kern,
        grid=(n_tiles + 2,),
        in_specs=[
            pl.BlockSpec(memory_space=pltpu.SMEM),
            pl.BlockSpec((4, HEAD_DIM), lambda s: (0, 0)),
            pl.BlockSpec((HEAD_WIDTH, 1), lambda s: (0, 0)),
            pl.BlockSpec((1, HEAD_WIDTH, tq), q_index),
            pl.BlockSpec((1, 1, seq, HEAD_WIDTH), kv_index(0)),
            pl.BlockSpec((1, 1, seq, HEAD_WIDTH), kv_index(1)),
        ],
        out_specs=pl.BlockSpec((tq, HEAD_WIDTH), out_index),
        out_shape=jax.ShapeDtypeStruct((T, n_heads * HEAD_WIDTH), _BF16),
        scratch_shapes=[
            pltpu.VMEM((2, n_kb, tk, HEAD_WIDTH), _BF16),
            pltpu.VMEM((2, n_kb, HEAD_WIDTH, tk), _BF16),
            pltpu.VMEM((tk // tq, tk, tq), _F32),
            scores, scores, col_stat, col_stat,
            acc, acc, col_stat, col_stat,
        ],
        compiler_params=_params("arbitrary"),
        name="diff_attn",
    )(slopes, lam_params, gain_col, q_t, kv, kv)


def _merge_kernel(o_ref, pz_ref, pzp_ref, pzn_ref, gz_ref, x_ref, bg_ref, pw_ref, ps_ref,
                  wa_ref, wp_ref, wo_ref, g_ref, out_ref, ext_ref, yp_ref, *, tm, seq):
    D = x_ref.shape[1]
    gdim = pw_ref.shape[1]
    rows = tm + 2 * HALO
    t0 = (pl.program_id(0) * tm) % seq
    prev = pzp_ref[...].astype(_F32)
    nxt = pzn_ref[...].astype(_F32)
    ext_ref[0:HALO] = jnp.where(t0 == 0, jnp.zeros_like(prev), prev)
    ext_ref[HALO:HALO + tm] = pz_ref[...].astype(_F32)
    ext_ref[HALO + tm:rows] = jnp.where(t0 + tm == seq, jnp.zeros_like(nxt), nxt)

    def ahead(a, k):
        return pltpu.roll(a, rows - k, axis=0)

    def behind(a, k):
        return pltpu.roll(a, k, axis=0)

    tpos = t0 + lax.broadcasted_iota(jnp.int32, (tm, 1), 0)
    for g, w in enumerate(POOL_WINDOWS):
        half = w // 2
        cols = slice(g * gdim, (g + 1) * gdim)
        e = ext_ref[:, cols]
        run, n = e, 1
        while n < half:
            run = run + ahead(run, n)
            n *= 2
        if half % F32_SUBLANES == 0:
            win = run[HALO - half:HALO - half + tm] + run[HALO:HALO + tm]
        else:
            win = (behind(run, half) + run)[HALO:HALO + tm]
        cnt = (jnp.minimum(tpos + half, seq) - jnp.maximum(tpos - half, 0)).astype(_F32)
        mixed = win * (1.0 / cnt) - e[HALO:HALO + tm]
        yg = jnp.dot(mixed.astype(_BF16), pw_ref[g], preferred_element_type=_F32)
        yp_ref[:, cols] = (yg * ps_ref[:, cols]).astype(_BF16)

    y_attn = jnp.dot(o_ref[...], wa_ref[...], preferred_element_type=_F32)
    y_pool = jnp.dot(yp_ref[...], wp_ref[...], preferred_element_type=_F32)
    gates = jax.nn.sigmoid(gz_ref[...].astype(_F32) + bg_ref[...])
    merged = gates[:, :D] * y_attn + gates[:, D:] * y_pool
    mo = jnp.dot(merged.astype(_BF16), wo_ref[...], preferred_element_type=_F32)
    out_ref[...] = x_ref[...] + _rms(mo, g_ref[...], NORM_EPS)


def _const_spec(shape):
    nd = len(shape)
    return pl.BlockSpec(shape, lambda i: (0,) * nd, pipeline_mode=pl.Buffered(1))


def _merge(o, pz, gz, x2, b_gate, pool_w, pool_scale, w_attn, w_pool, w_out, g_post, *, seq, tm):
    T, D = x2.shape
    aw = o.shape[1]
    hb = tm // HALO
    n_hb = T // HALO
    return pl.pallas_call(
        functools.partial(_merge_kernel, tm=tm, seq=seq),
        grid=(T // tm,),
        in_specs=[
            pl.BlockSpec((tm, aw), lambda i: (i, 0)),
            pl.BlockSpec((tm, aw), lambda i: (i, 0)),
            pl.BlockSpec((HALO, aw), lambda i: (jnp.maximum(i * hb - 1, 0), 0)),
            pl.BlockSpec((HALO, aw), lambda i: (jnp.minimum((i + 1) * hb, n_hb - 1), 0)),
            pl.BlockSpec((tm, 2 * D), lambda i: (i, 0)),
            pl.BlockSpec((tm, D), lambda i: (i, 0)),
            _const_spec(b_gate.shape),
            _const_spec(pool_w.shape),
            _const_spec(pool_scale.shape),
            _const_spec(w_attn.shape),
            _const_spec(w_pool.shape),
            _const_spec(w_out.shape),
            _const_spec(g_post.shape),
        ],
        out_specs=pl.BlockSpec((tm, D), lambda i: (i, 0)),
        out_shape=jax.ShapeDtypeStruct((T, D), _F32),
        scratch_shapes=[
            pltpu.VMEM((tm + 2 * HALO, aw), _F32),
            pltpu.VMEM((tm, aw), _BF16),
        ],
        compiler_params=_params("arbitrary"),
        name="merge",
    )(o, pz, pz, pz, gz, x2, b_gate, pool_w, pool_scale, w_attn, w_pool, w_out, g_post)


def _ffn_kernel(x_ref, xp_ref, xn_ref, g_ref, wg_ref, wv_ref, cwg_ref, cwv_ref, cbg_ref, cbv_ref,
                wd_ref, gp_ref, out_ref, h_ref, acc_ref, *, tm, seq):
    c = pl.program_id(1)
    rows = tm + 2 * HALO

    @pl.when(c == 0)
    def _():
        t0 = (pl.program_id(0) * tm) % seq
        g = g_ref[...]
        hp = _rms(xp_ref[...], g, NORM_EPS)
        hn = _rms(xn_ref[...], g, NORM_EPS)
        h_ref[0:HALO] = jnp.where(t0 == 0, jnp.zeros_like(hp), hp).astype(_BF16)
        h_ref[HALO:HALO + tm] = _rms(x_ref[...], g, NORM_EPS).astype(_BF16)
        h_ref[HALO + tm:rows] = jnp.where(t0 + tm == seq, jnp.zeros_like(hn), hn).astype(_BF16)
        acc_ref[...] = jnp.zeros_like(acc_ref)

    h = h_ref[...]

    def conv(w_ref, cw_ref, cb_ref):
        u = jnp.dot(h, w_ref[...], preferred_element_type=_F32)
        cw = cw_ref[...]
        u_prev = pltpu.roll(u, 1, axis=0)[HALO:HALO + tm]
        u_next = pltpu.roll(u, rows - 1, axis=0)[HALO:HALO + tm]
        return u_prev * cw[0:1] + u[HALO:HALO + tm] * cw[1:2] + u_next * cw[2:3] + cb_ref[...]

    gate = conv(wg_ref, cwg_ref, cbg_ref)
    val = conv(wv_ref, cwv_ref, cbv_ref)
    act = (jax.nn.gelu(gate) * val).astype(_BF16)
    acc_ref[...] += jnp.dot(act, wd_ref[...], preferred_element_type=_F32)

    @pl.when(c == pl.num_programs(1) - 1)
    def _():
        def store(rows, y):
            out_ref[rows] = x_ref[rows] + y

        _rms_rows(acc_ref, gp_ref[...], NORM_EPS, store)


def _ffn(x1, g_pre, w_up, conv_w, conv_b, w_down, g_post, *, seq, tm, tf):
    T, D = x1.shape
    d_ff = w_down.shape[0]
    n_c = d_ff // tf
    hb = tm // HALO
    n_hb = T // HALO
    return pl.pallas_call(
        functools.partial(_ffn_kernel, tm=tm, seq=seq),
        grid=(T // tm, n_c),
        in_specs=[
            pl.BlockSpec((tm, D), lambda i, c: (i, 0)),
            pl.BlockSpec((HALO, D), lambda i, c: (jnp.maximum(i * hb - 1, 0), 0)),
            pl.BlockSpec((HALO, D), lambda i, c: (jnp.minimum((i + 1) * hb, n_hb - 1), 0)),
            pl.BlockSpec((1, D), lambda i, c: (0, 0)),
            pl.BlockSpec((D, tf), lambda i, c: (0, c)),
            pl.BlockSpec((D, tf), lambda i, c: (0, n_c + c)),
            pl.BlockSpec((CONV_WIDTH, tf), lambda i, c: (0, c)),
            pl.BlockSpec((CONV_WIDTH, tf), lambda i, c: (0, n_c + c)),
            pl.BlockSpec((1, tf), lambda i, c: (0, c)),
            pl.BlockSpec((1, tf), lambda i, c: (0, n_c + c)),
            pl.BlockSpec((tf, D), lambda i, c: (c, 0)),
            pl.BlockSpec((1, D), lambda i, c: (0, 0)),
        ],
        out_specs=pl.BlockSpec((tm, D), lambda i, c: (i, 0)),
        out_shape=jax.ShapeDtypeStruct((T, D), _F32),
        scratch_shapes=[
            pltpu.VMEM((tm + 2 * HALO, D), _BF16),
            pltpu.VMEM((tm, D), _F32),
        ],
        compiler_params=_params("arbitrary", "arbitrary"),
        name="conv_ffn",
    )(x1, x1, x1, g_pre, w_up, w_up, conv_w, conv_w, conv_b, conv_b, w_down, g_post)


def _pick(n, pref):
    t = min(n, pref)
    while n % t:
        t //= 2
    return t


def kernel(x, g_pre_mix, w_in, b_gate, lambda_q1, lambda_k1, lambda_q2, lambda_k2, subln_gain,
           w_attn_branch, pool_w, pool_scale, w_pool_branch, w_out, g_post_mix, g_pre_ffn, w_up,
           conv_w, conv_b, w_down, g_post_ffn):
    B, S, D = x.shape
    depth = w_in.shape[0]
    aw = w_attn_branch.shape[1]
    n_heads = aw // HEAD_WIDTH
    d_ff = w_down.shape[1]
    T = B * S
    slopes_np = np.array([2.0 ** (-8.0 * (h + 1) / n_heads) for h in range(n_heads)], dtype=np.float32)
    assert np.all(np.frexp(slopes_np)[0] == 0.5)
    slopes = jnp.asarray(slopes_np)
    row = lambda a: a.reshape(1, -1).astype(_F32)

    x2 = x.reshape(T, D)
    for l in range(depth):
        lambda_init = 0.8 - 0.6 * math.exp(-0.3 * l)
        q_t, kv, pz, gz = _in_proj(x2, row(g_pre_mix[l]), w_in[l].astype(_BF16),
                                   n_heads=n_heads, tm=_pick(T, 1024))
        lam_params = jnp.stack([lambda_q1[l], lambda_k1[l], lambda_q2[l], lambda_k2[l]]).astype(_F32)
        o = _attention(q_t, kv, slopes, lam_params, subln_gain[l].reshape(-1, 1).astype(_F32),
                       batch=B, seq=S, tq=_pick(S, 256), tk=_pick(S, 512), lambda_init=lambda_init)
        x2 = _merge(o, pz, gz, x2, row(b_gate[l]), pool_w[l].astype(_BF16), row(pool_scale[l]),
                    w_attn_branch[l].astype(_BF16), w_pool_branch[l].astype(_BF16),
                    w_out[l].astype(_BF16), row(g_post_mix[l]), seq=S, tm=_pick(S, 256))
        x2 = _ffn(x2, row(g_pre_ffn[l]), w_up[l].astype(_BF16), conv_w[l].astype(_F32),
                  row(conv_b[l]), w_down[l].astype(_BF16), row(g_post_ffn[l]),
                  seq=S, tm=_pick(S, 512), tf=_pick(d_ff, 512))
    return x2.reshape(B, S, D)
```

```python
import functools
import math

import jax
import jax.numpy as jnp
import numpy as np
from jax import lax
from jax.experimental import pallas as pl
from jax.experimental.pallas import tpu as pltpu

HEAD_DIM = 64
HEAD_WIDTH = 2 * HEAD_DIM
POOL_WINDOWS = (2, 4, 8, 16)
CONV_WIDTH = 3
NORM_EPS = 1e-6
SUBLN_EPS = 1e-5
HALO = 16
F32_SUBLANES = 8
MXU_COLS = 256
POS_RADIX = 64
LOG2E = math.log2(math.e)
V7X_VMEM_LIMIT_BYTES = 56 * 1024 * 1024

_F32 = jnp.float32
_BF16 = jnp.bfloat16


def _rms(x, g, eps):
    return x * lax.rsqrt(jnp.mean(x * x, axis=-1, keepdims=True) + eps) * g


def _rms_rows(src, g, eps, store):
    for r0 in range(0, src.shape[0], F32_SUBLANES):
        rows = slice(r0, r0 + F32_SUBLANES)
        store(rows, _rms(src[rows], g, eps))


def _divmod(x, n):
    if n & (n - 1) == 0:
        return x >> (n.bit_length() - 1), x & (n - 1)
    return x // n, x % n


def _params(*sem):
    return pltpu.CompilerParams(dimension_semantics=sem, vmem_limit_bytes=V7X_VMEM_LIMIT_BYTES)


def _in_proj_kernel(x_ref, g_ref, w_ref, qt_ref, k_ref, vt_ref, pz_ref, gz_ref, h_ref, *, n_heads):
    j = pl.program_id(1)

    @pl.when(j == 0)
    def _():
        h_ref[...] = _rms(x_ref[...], g_ref[...], NORM_EPS).astype(_BF16)

    def project(store):
        h = h_ref[...]
        for lo in range(0, w_ref.shape[1], MXU_COLS):
            store(lo, jnp.dot(h, w_ref[:, lo:lo + MXU_COLS], preferred_element_type=_F32))

    def heads_of(lo, z):
        for k in range(MXU_COLS // HEAD_WIDTH):
            yield lo // HEAD_WIDTH + k, z[:, k * HEAD_WIDTH:(k + 1) * HEAD_WIDTH]

    def store_q(lo, z):
        for head, z_h in heads_of(lo, z):
            qt_ref[head] = (z_h * (HEAD_DIM ** -0.5)).T.astype(_BF16)

    def store_k(lo, z):
        for head, z_h in heads_of(lo, z):
            k_ref[head] = z_h.astype(_BF16)

    def store_v(lo, z):
        for head, z_h in heads_of(lo, z):
            vt_ref[head] = z_h.T.astype(_BF16)

    def store_pz(lo, z):
        pz_ref[:, lo:lo + MXU_COLS] = z.astype(_BF16)

    def store_gz(lo, z):
        gz_ref[:, lo:lo + MXU_COLS] = z.astype(_BF16)

    @pl.when(j == 0)
    def _():
        project(store_q)

    @pl.when(j == 1)
    def _():
        project(store_k)

    @pl.when(j == 2)
    def _():
        project(store_v)

    @pl.when(j == 3)
    def _():
        project(store_pz)

    @pl.when(j > 3)
    def _():
        project(store_gz)


def _in_proj(x2, g, w, *, n_heads, tm):
    T, D = x2.shape
    aw = n_heads * HEAD_WIDTH
    n_gate_blocks = 2 * D // aw
    grid = (T // tm, 4 + n_gate_blocks)
    return pl.pallas_call(
        functools.partial(_in_proj_kernel, n_heads=n_heads),
        grid=grid,
        in_specs=[
            pl.BlockSpec((tm, D), lambda i, j: (i, 0)),
            pl.BlockSpec((1, D), lambda i, j: (0, 0)),
            pl.BlockSpec((D, aw), lambda i, j: (0, j)),
        ],
        out_specs=[
            pl.BlockSpec((n_heads, HEAD_WIDTH, tm), lambda i, j: (0, 0, i)),
            pl.BlockSpec((n_heads, tm, HEAD_WIDTH), lambda i, j: (0, i, 0)),
            pl.BlockSpec((n_heads, HEAD_WIDTH, tm), lambda i, j: (0, 0, i)),
            pl.BlockSpec((tm, aw), lambda i, j: (i, 0)),
            pl.BlockSpec((tm, aw), lambda i, j: (i, jnp.maximum(j - 4, 0))),
        ],
        out_shape=[
            jax.ShapeDtypeStruct((n_heads, HEAD_WIDTH, T), _BF16),
            jax.ShapeDtypeStruct((n_heads, T, HEAD_WIDTH), _BF16),
            jax.ShapeDtypeStruct((n_heads, HEAD_WIDTH, T), _BF16),
            jax.ShapeDtypeStruct((T, aw), _BF16),
            jax.ShapeDtypeStruct((T, 2 * D), _BF16),
        ],
        scratch_shapes=[pltpu.VMEM((tm, D), _BF16)],
        compiler_params=_params("arbitrary", "arbitrary"),
        name="in_proj",
    )(x2, g, w)


def _pos_features(pos, idx, base, values):
    hi = (pos // POS_RADIX).astype(_F32)
    lo = (pos % POS_RADIX).astype(_F32)
    out = jnp.zeros(pos.shape, _F32)
    for n, val in enumerate(values(hi, lo)):
        out = jnp.where(idx == base + n, val, out)
    return out


def _attn_kernel(slopes_ref, lam_ref, gain_ref, q_ref, k_ref, vt_ref, o_ref,
                 kaug_ref, corr_ref, s_even_ref, s_odd_ref, m_even_ref, m_odd_ref,
                 acc_even_ref, acc_odd_ref, l_even_ref, l_odd_ref,
                 *, tq, tk, n_kb, n_q, n_heads, n_tiles, lambda_init):
    step = pl.program_id(0)
    tile = jnp.minimum(step, n_tiles - 1)
    head, qi = _divmod(tile, n_q)
    feat_base = (HEAD_DIM, 0)

    @pl.when(step == 0)
    def _():
        s_odd_ref[...] = jnp.zeros_like(s_odd_ref)
        m_odd_ref[...] = jnp.zeros_like(m_odd_ref)
        acc_odd_ref[...] = jnp.zeros_like(acc_odd_ref)
        l_odd_ref[...] = jnp.ones_like(l_odd_ref)

    @pl.when((qi == 0) & (step < n_tiles))
    def _():
        slope = slopes_ref[_divmod(head, n_heads)[1]]
        lane = lax.broadcasted_iota(jnp.int32, (tk, HEAD_WIDTH), 1)
        row = lax.broadcasted_iota(jnp.int32, (tk, HEAD_WIDTH), 0)
        feats, feats_per_block, own_half = [], [], []
        for c in range(2):
            feats.append(_pos_features(
                row, lane, feat_base[c],
                lambda hi, lo: (-POS_RADIX * slope, -slope, POS_RADIX * slope * hi, slope * lo)))
            feats_per_block.append(jnp.where(lane == feat_base[c] + 2, slope * tk, 0.0))
            own_half.append((lane >= c * HEAD_DIM) & (lane < (c + 1) * HEAD_DIM))
        for jb in range(n_kb):
            kb = k_ref[0, jb * tk:(jb + 1) * tk, :].astype(_F32)
            for c in range(2):
                kaug_ref[c, jb] = jnp.where(own_half[c], kb, feats[c]).astype(_BF16)
                feats[c] = feats[c] + feats_per_block[c]
        rel = (lax.broadcasted_iota(jnp.int32, (tk, tq), 0)
               - lax.broadcasted_iota(jnp.int32, (tk, tq), 1))
        for v in range(tk // tq):
            corr_ref[v] = (2.0 * slope) * jnp.maximum(rel - v * tq, 0).astype(_F32)

    q_t = q_ref[0]
    q_half = (q_t[:HEAD_DIM], q_t[HEAD_DIM:])
    sub = lax.broadcasted_iota(jnp.int32, (HALO, tq), 0)
    qfeat = _pos_features(qi * tq + lax.broadcasted_iota(jnp.int32, (HALO, tq), 1),
                          sub, 0, lambda hi, lo: (hi, lo, 1.0, 1.0)).astype(_BF16)
    no_feat = jnp.zeros((HEAD_DIM - HALO, tq), _BF16)

    def with_features(c, f):
        other_half = jnp.concatenate([f, no_feat], axis=0)
        parts = (q_half[0], other_half) if c == 0 else (other_half, q_half[1])
        return jnp.concatenate(parts, axis=0)

    q_left = [with_features(c, qfeat) for c in range(2)]
    q_right = [with_features(c, -qfeat) for c in range(2)]

    jd, offset_in_block = _divmod(qi * tq, tk)
    correction = corr_ref[_divmod(offset_in_block, tq)[0]]

    lam_p = lam_ref[...]
    lam = (jnp.exp(jnp.sum(lam_p[0:1] * lam_p[1:2], axis=-1, keepdims=True))
           - jnp.exp(jnp.sum(lam_p[2:3] * lam_p[3:4], axis=-1, keepdims=True))
           + lambda_init)

    def run_step(s_new_ref, m_new_ref, acc_new_ref, l_new_ref,
                 s_old_ref, m_old_ref, acc_old_ref, l_old_ref):
        l_old = [jnp.sum(l_old_ref[c], axis=0, keepdims=True) for c in range(2)]
        o_t = acc_old_ref[0] / l_old[0] - lam * (acc_old_ref[1] / l_old[1])
        ms = jnp.mean(o_t * o_t, axis=0, keepdims=True)
        o_t = o_t * lax.rsqrt(ms + SUBLN_EPS) * gain_ref[...] * (1.0 - lambda_init)
        o_ref[...] = o_t.T.astype(o_ref.dtype)

        def score_block(d, col_max):
            jb = jd if d == 0 else _divmod(jd + d, n_kb)[1]
            out = []
            for c in range(2):
                q_c = q_left[c] if d == 0 else jnp.where(jb < jd, q_left[c], q_right[c])
                s = jnp.dot(kaug_ref[c, jb], q_c, preferred_element_type=_F32)
                if d == 0:
                    s = s - correction
                s = s * LOG2E
                s_new_ref[c, jb] = s
                block_max = jnp.max(s.reshape(tk // 8, 8, tq), axis=0)
                out.append(block_max if col_max is None else jnp.maximum(col_max[c], block_max))
            return out

        m_old = [jnp.max(m_old_ref[c], axis=0, keepdims=True) for c in range(2)]
        col_sum = [jnp.zeros((8, tq), _F32), jnp.zeros((8, tq), _F32)]
        col_max = None
        for jb in range(n_kb):
            col_max = score_block(jb, col_max)
            v_t = vt_ref[0, :, jb * tk:(jb + 1) * tk]
            for c in range(2):
                p = jnp.exp2(s_old_ref[c, jb] - m_old[c])
                col_sum[c] = col_sum[c] + jnp.sum(p.reshape(tk // 8, 8, tq), axis=0)
                pv = jnp.dot(v_t, p.astype(_BF16), preferred_element_type=_F32)
                if jb == 0:
                    acc_new_ref[c] = pv
                else:
                    acc_new_ref[c] += pv
        for c in range(2):
            m_new_ref[c] = col_max[c]
            l_new_ref[c] = col_sum[c]

    even = (s_even_ref, m_even_ref, acc_even_ref, l_even_ref)
    odd = (s_odd_ref, m_odd_ref, acc_odd_ref, l_odd_ref)

    @pl.when(step % 2 == 0)
    def _():
        run_step(*even, *odd)

    @pl.when(step % 2 == 1)
    def _():
        run_step(*odd, *even)


def _attention(q_t, k, v_t, slopes, lam_params, gain_col, *, batch, seq, tq, tk, lambda_init):
    n_heads, T, _ = k.shape
    n_q = seq // tq
    n_kb = seq // tk
    n_tiles = batch * n_heads * n_q
    assert tk % tq == 0 and seq <= POS_RADIX * 256
    kern = functools.partial(_attn_kernel, tq=tq, tk=tk, n_kb=n_kb, n_q=n_q, n_heads=n_heads,
                             n_tiles=n_tiles, lambda_init=lambda_init)

    def score_tile(s):
        t = jnp.minimum(s, n_tiles - 1)
        head, i = _divmod(t, n_q)
        b, h = _divmod(head, n_heads)
        return b, h, i

    def output_tile(s):
        return score_tile(jnp.maximum(s - 2, 0))

    def q_index(s):
        b, h, i = score_tile(s)
        return h, 0, b * n_q + i

    def k_index(s):
        b, h, _ = score_tile(s)
        return h, b, 0

    def v_index(s):
        b, h, _ = score_tile(jnp.maximum(s - 1, 0))
        return h, 0, b

    def out_index(s):
        b, h, i = output_tile(s)
        return b * n_q + i, h

    scores = pltpu.VMEM((2, n_kb, tk, tq), _F32)
    col_stat = pltpu.VMEM((2, F32_SUBLANES, tq), _F32)
    acc = pltpu.VMEM((2, HEAD_WIDTH, tq), _F32)
    return pl.pallas_call(
        kern,
        grid=(n_tiles + 2,),
        in_specs=[
            pl.BlockSpec(memory_space=pltpu.SMEM),
            pl.BlockSpec((4, HEAD_DIM), lambda s: (0, 0)),
            pl.BlockSpec((HEAD_WIDTH, 1), lambda s: (0, 0)),
            pl.BlockSpec((1, HEAD_WIDTH, tq), q_index),
            pl.BlockSpec((1, seq, HEAD_WIDTH), k_index),
            pl.BlockSpec((1, HEAD_WIDTH, seq), v_index),
        ],
        out_specs=pl.BlockSpec((tq, HEAD_WIDTH), out_index),
        out_shape=jax.ShapeDtypeStruct((T, n_heads * HEAD_WIDTH), _BF16),
        scratch_shapes=[
            pltpu.VMEM((2, n_kb, tk, HEAD_WIDTH), _BF16),
            pltpu.VMEM((tk // tq, tk, tq), _F32),
            scores, scores, col_stat, col_stat,
            acc, acc, col_stat, col_stat,
        ],
        compiler_params=_params("arbitrary"),
        name="diff_attn",
    )(slopes, lam_params, gain_col, q_t, k, v_t)


def _merge_kernel(o_ref, pz_ref, pzp_ref, pzn_ref, gz_ref, x_ref, bg_ref, pw_ref, ps_ref,
                  wa_ref, wp_ref, wo_ref, g_ref, out_ref, ext_ref, yp_ref, *, tm, seq):
    D = x_ref.shape[1]
    gdim = pw_ref.shape[1]
    rows = tm + 2 * HALO
    t0 = (pl.program_id(0) * tm) % seq
    prev = pzp_ref[...].astype(_F32)
    nxt = pzn_ref[...].astype(_F32)
    ext_ref[0:HALO] = jnp.where(t0 == 0, jnp.zeros_like(prev), prev)
    ext_ref[HALO:HALO + tm] = pz_ref[...].astype(_F32)
    ext_ref[HALO + tm:rows] = jnp.where(t0 + tm == seq, jnp.zeros_like(nxt), nxt)

    def ahead(a, k):
        return pltpu.roll(a, rows - k, axis=0)

    def behind(a, k):
        return pltpu.roll(a, k, axis=0)

    tpos = t0 + lax.broadcasted_iota(jnp.int32, (tm, 1), 0)
    for g, w in enumerate(POOL_WINDOWS):
        half = w // 2
        cols = slice(g * gdim, (g + 1) * gdim)
        e = ext_ref[:, cols]
        run, n = e, 1
        while n < half:
            run = run + ahead(run, n)
            n *= 2
        if half % F32_SUBLANES == 0:
            win = run[HALO - half:HALO - half + tm] + run[HALO:HALO + tm]
        else:
            win = (behind(run, half) + run)[HALO:HALO + tm]
        cnt = (jnp.minimum(tpos + half, seq) - jnp.maximum(tpos - half, 0)).astype(_F32)
        mixed = win * (1.0 / cnt) - e[HALO:HALO + tm]
        yg = jnp.dot(mixed.astype(_BF16), pw_ref[g], preferred_element_type=_F32)
        yp_ref[:, cols] = (yg * ps_ref[:, cols]).astype(_BF16)

    y_attn = jnp.dot(o_ref[...], wa_ref[...], preferred_element_type=_F32)
    y_pool = jnp.dot(yp_ref[...], wp_ref[...], preferred_element_type=_F32)
    gates = jax.nn.sigmoid(gz_ref[...].astype(_F32) + bg_ref[...])
    merged = gates[:, :D] * y_attn + gates[:, D:] * y_pool
    mo = jnp.dot(merged.astype(_BF16), wo_ref[...], preferred_element_type=_F32)
    out_ref[...] = x_ref[...] + _rms(mo, g_ref[...], NORM_EPS)


def _const_spec(shape):
    nd = len(shape)
    return pl.BlockSpec(shape, lambda i: (0,) * nd, pipeline_mode=pl.Buffered(1))


def _merge(o, pz, gz, x2, b_gate, pool_w, pool_scale, w_attn, w_pool, w_out, g_post, *, seq, tm):
    T, D = x2.shape
    aw = o.shape[1]
    hb = tm // HALO
    n_hb = T // HALO
    return pl.pallas_call(
        functools.partial(_merge_kernel, tm=tm, seq=seq),
        grid=(T // tm,),
        in_specs=[
            pl.BlockSpec((tm, aw), lambda i: (i, 0)),
            pl.BlockSpec((tm, aw), lambda i: (i, 0)),
            pl.BlockSpec((HALO, aw), lambda i: (jnp.maximum(i * hb - 1, 0), 0)),
            pl.BlockSpec((HALO, aw), lambda i: (jnp.minimum((i + 1) * hb, n_hb - 1), 0)),
            pl.BlockSpec((tm, 2 * D), lambda i: (i, 0)),
            pl.BlockSpec((tm, D), lambda i: (i, 0)),
            _const_spec(b_gate.shape),
            _const_spec(pool_w.shape),
            _const_spec(pool_scale.shape),
            _const_spec(w_attn.shape),
            _const_spec(w_pool.shape),
            _const_spec(w_out.shape),
            _const_spec(g_post.shape),
        ],
        out_specs=pl.BlockSpec((tm, D), lambda i: (i, 0)),
        out_shape=jax.ShapeDtypeStruct((T, D), _F32),
        scratch_shapes=[
            pltpu.VMEM((tm + 2 * HALO, aw), _F32),
            pltpu.VMEM((tm, aw), _BF16),
        ],
        compiler_params=_params("arbitrary"),
        name="merge",
    )(o, pz, pz, pz, gz, x2, b_gate, pool_w, pool_scale, w_attn, w_pool, w_out, g_post)


def _ffn_kernel(x_ref, xp_ref, xn_ref, g_ref, wg_ref, wv_ref, cwg_ref, cwv_ref, cbg_ref, cbv_ref,
                wd_ref, gp_ref, out_ref, h_ref, acc_ref, *, tm, seq):
    c = pl.program_id(1)
    rows = tm + 2 * HALO

    @pl.when(c == 0)
    def _():
        t0 = (pl.program_id(0) * tm) % seq
        g = g_ref[...]
        hp = _rms(xp_ref[...], g, NORM_EPS)
        hn = _rms(xn_ref[...], g, NORM_EPS)
        h_ref[0:HALO] = jnp.where(t0 == 0, jnp.zeros_like(hp), hp).astype(_BF16)
        h_ref[HALO:HALO + tm] = _rms(x_ref[...], g, NORM_EPS).astype(_BF16)
        h_ref[HALO + tm:rows] = jnp.where(t0 + tm == seq, jnp.zeros_like(hn), hn).astype(_BF16)
        acc_ref[...] = jnp.zeros_like(acc_ref)

    h = h_ref[...]

    def conv(w_ref, cw_ref, cb_ref):
        u = jnp.dot(h, w_ref[...], preferred_element_type=_F32)
        cw = cw_ref[...]
        u_prev = pltpu.roll(u, 1, axis=0)[HALO:HALO + tm]
        u_next = pltpu.roll(u, rows - 1, axis=0)[HALO:HALO + tm]
        return u_prev * cw[0:1] + u[HALO:HALO + tm] * cw[1:2] + u_next * cw[2:3] + cb_ref[...]

    gate = conv(wg_ref, cwg_ref, cbg_ref)
    val = conv(wv_ref, cwv_ref, cbv_ref)
    act = (jax.nn.gelu(gate) * val).astype(_BF16)
    acc_ref[...] += jnp.dot(act, wd_ref[...], preferred_element_type=_F32)

    @pl.when(c == pl.num_programs(1) - 1)
    def _():
        def store(rows, y):
            out_ref[rows] = x_ref[rows] + y

        _rms_rows(acc_ref, gp_ref[...], NORM_EPS, store)


def _ffn(x1, g_pre, w_up, conv_w, conv_b, w_down, g_post, *, seq, tm, tf):
    T, D = x1.shape
    d_ff = w_down.shape[0]
    n_c = d_ff // tf
    hb = tm // HALO
    n_hb = T // HALO
    return pl.pallas_call(
        functools.partial(_ffn_kernel, tm=tm, seq=seq),
        grid=(T // tm, n_c),
        in_specs=[
            pl.BlockSpec((tm, D), lambda i, c: (i, 0)),
            pl.BlockSpec((HALO, D), lambda i, c: (jnp.maximum(i * hb - 1, 0), 0)),
            pl.BlockSpec((HALO, D), lambda i, c: (jnp.minimum((i + 1) * hb, n_hb - 1), 0)),
            pl.BlockSpec((1, D), lambda i, c: (0, 0)),
            pl.BlockSpec((D, tf), lambda i, c: (0, c)),
            pl.BlockSpec((D, tf), lambda i, c: (0, n_c + c)),
            pl.BlockSpec((CONV_WIDTH, tf), lambda i, c: (0, c)),
            pl.BlockSpec((CONV_WIDTH, tf), lambda i, c: (0, n_c + c)),
            pl.BlockSpec((1, tf), lambda i, c: (0, c)),
            pl.BlockSpec((1, tf), lambda i, c: (0, n_c + c)),
            pl.BlockSpec((tf, D), lambda i, c: (c, 0)),
            pl.BlockSpec((1, D), lambda i, c: (0, 0)),
        ],
        out_specs=pl.BlockSpec((tm, D), lambda i, c: (i, 0)),
        out_shape=jax.ShapeDtypeStruct((T, D), _F32),
        scratch_shapes=[
            pltpu.VMEM((tm + 2 * HALO, D), _BF16),
            pltpu.VMEM((tm, D), _F32),
        ],
        compiler_params=_params("arbitrary", "arbitrary"),
        name="conv_ffn",
    )(x1, x1, x1, g_pre, w_up, w_up, conv_w, conv_w, conv_b, conv_b, w_down, g_post)


def _pick(n, pref):
    t = min(n, pref)
    while n % t:
        t //= 2
    return t


def kernel(x, g_pre_mix, w_in, b_gate, lambda_q1, lambda_k1, lambda_q2, lambda_k2, subln_gain,
           w_attn_branch, pool_w, pool_scale, w_pool_branch, w_out, g_post_mix, g_pre_ffn, w_up,
           conv_w, conv_b, w_down, g_post_ffn):
    B, S, D = x.shape
    depth = w_in.shape[0]
    aw = w_attn_branch.shape[1]
    n_heads = aw // HEAD_WIDTH
    d_ff = w_down.shape[1]
    T = B * S
    slopes_np = np.array([2.0 ** (-8.0 * (h + 1) / n_heads) for h in range(n_heads)], dtype=np.float32)
    assert np.all(np.frexp(slopes_np)[0] == 0.5)
    slopes = jnp.asarray(slopes_np)
    row = lambda a: a.reshape(1, -1).astype(_F32)

    x2 = x.reshape(T, D)
    for l in range(depth):
        lambda_init = 0.8 - 0.6 * math.exp(-0.3 * l)
        q_t, k, v_t, pz, gz = _in_proj(x2, row(g_pre_mix[l]), w_in[l].astype(_BF16),
                                       n_heads=n_heads, tm=_pick(T, 1024))
        lam_params = jnp.stack([lambda_q1[l], lambda_k1[l], lambda_q2[l], lambda_k2[l]]).astype(_F32)
        o = _attention(q_t, k, v_t, slopes, lam_params, subln_gain[l].reshape(-1, 1).astype(_F32),
                       batch=B, seq=S, tq=_pick(S, 256), tk=_pick(S, 512), lambda_init=lambda_init)
        x2 = _merge(o, pz, gz, x2, row(b_gate[l]), pool_w[l].astype(_BF16), row(pool_scale[l]),
                    w_attn_branch[l].astype(_BF16), w_pool_branch[l].astype(_BF16),
                    w_out[l].astype(_BF16), row(g_post_mix[l]), seq=S, tm=_pick(S, 256))
        x2 = _ffn(x2, row(g_pre_ffn[l]), w_up[l].astype(_BF16), conv_w[l].astype(_F32),
                  row(conv_b[l]), w_down[l].astype(_BF16), row(g_post_ffn[l]),
                  seq=S, tm=_pick(S, 512), tf=_pick(d_ff, 512))
    return x2.reshape(B, S, D)
```

```python
import functools
import math

import jax
import jax.numpy as jnp
import numpy as np
from jax import lax
from jax.experimental import pallas as pl
from jax.experimental.pallas import tpu as pltpu

HEAD_DIM = 64
HEAD_WIDTH = 2 * HEAD_DIM
POOL_WINDOWS = (2, 4, 8, 16)
CONV_WIDTH = 3
NORM_EPS = 1e-6
SUBLN_EPS = 1e-5
HALO = 16
F32_SUBLANES = 8
MXU_COLS = 256
POS_RADIX = 64
LOG2E = math.log2(math.e)
V7X_VMEM_LIMIT_BYTES = 56 * 1024 * 1024

_F32 = jnp.float32
_BF16 = jnp.bfloat16


def _rms(x, g, eps):
    return x * lax.rsqrt(jnp.mean(x * x, axis=-1, keepdims=True) + eps) * g


def _rms_rows(src, g, eps, store):
    for r0 in range(0, src.shape[0], F32_SUBLANES):
        rows = slice(r0, r0 + F32_SUBLANES)
        store(rows, _rms(src[rows], g, eps))


def _divmod(x, n):
    if n & (n - 1) == 0:
        return x >> (n.bit_length() - 1), x & (n - 1)
    return x // n, x % n


def _params(*sem):
    return pltpu.CompilerParams(dimension_semantics=sem, vmem_limit_bytes=V7X_VMEM_LIMIT_BYTES)


def _in_proj_kernel(x_ref, g_ref, w_ref, qt_ref, k_ref, vt_ref, pz_ref, gz_ref, h_ref, *, n_heads):
    j = pl.program_id(1)

    @pl.when(j == 0)
    def _():
        h_ref[...] = _rms(x_ref[...], g_ref[...], NORM_EPS).astype(_BF16)

    def project(store):
        h = h_ref[...]
        for lo in range(0, w_ref.shape[1], MXU_COLS):
            store(lo, jnp.dot(h, w_ref[:, lo:lo + MXU_COLS], preferred_element_type=_F32))

    def heads_of(lo, z):
        for k in range(MXU_COLS // HEAD_WIDTH):
            yield lo // HEAD_WIDTH + k, z[:, k * HEAD_WIDTH:(k + 1) * HEAD_WIDTH]

    def store_q(lo, z):
        for head, z_h in heads_of(lo, z):
            qt_ref[head] = (z_h * (HEAD_DIM ** -0.5)).T.astype(_BF16)

    def store_k(lo, z):
        for head, z_h in heads_of(lo, z):
            k_ref[head] = z_h.astype(_BF16)

    def store_v(lo, z):
        for head, z_h in heads_of(lo, z):
            vt_ref[head] = z_h.T.astype(_BF16)

    def store_pz(lo, z):
        pz_ref[:, lo:lo + MXU_COLS] = z.astype(_BF16)

    def store_gz(lo, z):
        gz_ref[:, lo:lo + MXU_COLS] = z.astype(_BF16)

    @pl.when(j == 0)
    def _():
        project(store_q)

    @pl.when(j == 1)
    def _():
        project(store_k)

    @pl.when(j == 2)
    def _():
        project(store_v)

    @pl.when(j == 3)
    def _():
        project(store_pz)

    @pl.when(j > 3)
    def _():
        project(store_gz)


def _in_proj(x2, g, w, *, n_heads, tm):
    T, D = x2.shape
    aw = n_heads * HEAD_WIDTH
    n_gate_blocks = 2 * D // aw
    grid = (T // tm, 4 + n_gate_blocks)
    return pl.pallas_call(
        functools.partial(_in_proj_kernel, n_heads=n_heads),
        grid=grid,
        in_specs=[
            pl.BlockSpec((tm, D), lambda i, j: (i, 0)),
            pl.BlockSpec((1, D), lambda i, j: (0, 0)),
            pl.BlockSpec((D, aw), lambda i, j: (0, j)),
        ],
        out_specs=[
            pl.BlockSpec((n_heads, HEAD_WIDTH, tm), lambda i, j: (0, 0, i)),
            pl.BlockSpec((n_heads, tm, HEAD_WIDTH), lambda i, j: (0, i, 0)),
            pl.BlockSpec((n_heads, HEAD_WIDTH, tm), lambda i, j: (0, 0, i)),
            pl.BlockSpec((tm, aw), lambda i, j: (i, 0)),
            pl.BlockSpec((tm, aw), lambda i, j: (i, jnp.maximum(j - 4, 0))),
        ],
        out_shape=[
            jax.ShapeDtypeStruct((n_heads, HEAD_WIDTH, T), _BF16),
            jax.ShapeDtypeStruct((n_heads, T, HEAD_WIDTH), _BF16),
            jax.ShapeDtypeStruct((n_heads, HEAD_WIDTH, T), _BF16),
            jax.ShapeDtypeStruct((T, aw), _BF16),
            jax.ShapeDtypeStruct((T, 2 * D), _BF16),
        ],
        scratch_shapes=[pltpu.VMEM((tm, D), _BF16)],
        compiler_params=_params("arbitrary", "arbitrary"),
        name="in_proj",
    )(x2, g, w)


def _pos_features(pos, idx, base, values):
    hi = (pos // POS_RADIX).astype(_F32)
    lo = (pos % POS_RADIX).astype(_F32)
    out = jnp.zeros(pos.shape, _F32)
    for n, val in enumerate(values(hi, lo)):
        out = jnp.where(idx == base + n, val, out)
    return out


def _attn_kernel(slopes_ref, lam_ref, gain_ref, q_ref, k_ref, vt_ref, o_ref,
                 kaug_ref, corr_ref, s_even_ref, s_odd_ref, m_even_ref, m_odd_ref,
                 acc_even_ref, acc_odd_ref, l_even_ref, l_odd_ref,
                 *, tq, tk, n_kb, n_q, n_heads, n_tiles, lambda_init):
    step = pl.program_id(0)
    tile = jnp.minimum(step, n_tiles - 1)
    head, qi = _divmod(tile, n_q)
    feat_base = (HEAD_DIM, 0)

    @pl.when(step == 0)
    def _():
        s_odd_ref[...] = jnp.zeros_like(s_odd_ref)
        m_odd_ref[...] = jnp.zeros_like(m_odd_ref)
        acc_odd_ref[...] = jnp.zeros_like(acc_odd_ref)
        l_odd_ref[...] = jnp.ones_like(l_odd_ref)

    @pl.when((qi == 0) & (step < n_tiles))
    def _():
        slope = slopes_ref[_divmod(head, n_heads)[1]]
        lane = lax.broadcasted_iota(jnp.int32, (tk, HEAD_WIDTH), 1)
        row = lax.broadcasted_iota(jnp.int32, (tk, HEAD_WIDTH), 0)
        feats, feats_per_block, own_half = [], [], []
        for c in range(2):
            feats.append(_pos_features(
                row, lane, feat_base[c],
                lambda hi, lo: (-POS_RADIX * slope, -slope, POS_RADIX * slope * hi, slope * lo)))
            feats_per_block.append(jnp.where(lane == feat_base[c] + 2, slope * tk, 0.0))
            own_half.append((lane >= c * HEAD_DIM) & (lane < (c + 1) * HEAD_DIM))
        for jb in range(n_kb):
            kb = k_ref[0, jb * tk:(jb + 1) * tk, :].astype(_F32)
            for c in range(2):
                kaug_ref[c, jb] = jnp.where(own_half[c], kb, feats[c]).astype(_BF16)
                feats[c] = feats[c] + feats_per_block[c]
        rel = (lax.broadcasted_iota(jnp.int32, (tk, tq), 0)
               - lax.broadcasted_iota(jnp.int32, (tk, tq), 1))
        for v in range(tk // tq):
            corr_ref[v] = (2.0 * slope) * jnp.maximum(rel - v * tq, 0).astype(_F32)

    q_t = q_ref[0]
    q_half = (q_t[:HEAD_DIM], q_t[HEAD_DIM:])
    sub = lax.broadcasted_iota(jnp.int32, (HALO, tq), 0)
    qfeat = _pos_features(qi * tq + lax.broadcasted_iota(jnp.int32, (HALO, tq), 1),
                          sub, 0, lambda hi, lo: (hi, lo, 1.0, 1.0)).astype(_BF16)
    no_feat = jnp.zeros((HEAD_DIM - HALO, tq), _BF16)

    def with_features(c, f):
        other_half = jnp.concatenate([f, no_feat], axis=0)
        parts = (q_half[0], other_half) if c == 0 else (other_half, q_half[1])
        return jnp.concatenate(parts, axis=0)

    q_left = [with_features(c, qfeat) for c in range(2)]
    q_right = [with_features(c, -qfeat) for c in range(2)]

    jd, offset_in_block = _divmod(qi * tq, tk)
    correction = corr_ref[_divmod(offset_in_block, tq)[0]]

    lam_p = lam_ref[...]
    lam = (jnp.exp(jnp.sum(lam_p[0:1] * lam_p[1:2], axis=-1, keepdims=True))
           - jnp.exp(jnp.sum(lam_p[2:3] * lam_p[3:4], axis=-1, keepdims=True))
           + lambda_init)

    def run_step(s_new_ref, m_new_ref, acc_new_ref, l_new_ref,
                 s_old_ref, m_old_ref, acc_old_ref, l_old_ref):
        l_old = [jnp.sum(l_old_ref[c], axis=0, keepdims=True) for c in range(2)]
        o_t = acc_old_ref[0] / l_old[0] - lam * (acc_old_ref[1] / l_old[1])
        ms = jnp.mean(o_t * o_t, axis=0, keepdims=True)
        o_t = o_t * lax.rsqrt(ms + SUBLN_EPS) * gain_ref[...] * (1.0 - lambda_init)
        o_ref[0] = o_t.astype(o_ref.dtype)

        def score_block(d, col_max):
            jb = jd if d == 0 else _divmod(jd + d, n_kb)[1]
            out = []
            for c in range(2):
                q_c = q_left[c] if d == 0 else jnp.where(jb < jd, q_left[c], q_right[c])
                s = jnp.dot(kaug_ref[c, jb], q_c, preferred_element_type=_F32)
                if d == 0:
                    s = s - correction
                s = s * LOG2E
                s_new_ref[c, jb] = s
                block_max = jnp.max(s.reshape(tk // F32_SUBLANES, F32_SUBLANES, tq), axis=0)
                out.append(block_max if col_max is None else jnp.maximum(col_max[c], block_max))
            return out

        m_old = [jnp.max(m_old_ref[c], axis=0, keepdims=True) for c in range(2)]
        col_sum = [jnp.zeros((F32_SUBLANES, tq), _F32), jnp.zeros((F32_SUBLANES, tq), _F32)]
        col_max = None
        for jb in range(n_kb):
            col_max = score_block(jb, col_max)
            v_t = vt_ref[0, :, jb * tk:(jb + 1) * tk]
            for c in range(2):
                p = jnp.exp2(s_old_ref[c, jb] - m_old[c])
                col_sum[c] = col_sum[c] + jnp.sum(
                    p.reshape(tk // F32_SUBLANES, F32_SUBLANES, tq), axis=0)
                pv = jnp.dot(v_t, p.astype(_BF16), preferred_element_type=_F32)
                if jb == 0:
                    acc_new_ref[c] = pv
                else:
                    acc_new_ref[c] += pv
        for c in range(2):
            m_new_ref[c] = col_max[c]
            l_new_ref[c] = col_sum[c]

    even = (s_even_ref, m_even_ref, acc_even_ref, l_even_ref)
    odd = (s_odd_ref, m_odd_ref, acc_odd_ref, l_odd_ref)

    @pl.when(step % 2 == 0)
    def _():
        run_step(*even, *odd)

    @pl.when(step % 2 == 1)
    def _():
        run_step(*odd, *even)


def _attention(q_t, k, v_t, slopes, lam_params, gain_col, *, batch, seq, tq, tk, lambda_init):
    n_heads, T, _ = k.shape
    n_q = seq // tq
    n_kb = seq // tk
    n_tiles = batch * n_heads * n_q
    assert tk % tq == 0 and seq <= POS_RADIX * 256
    kern = functools.partial(_attn_kernel, tq=tq, tk=tk, n_kb=n_kb, n_q=n_q, n_heads=n_heads,
                             n_tiles=n_tiles, lambda_init=lambda_init)

    def score_tile(s):
        t = jnp.minimum(s, n_tiles - 1)
        head, i = _divmod(t, n_q)
        b, h = _divmod(head, n_heads)
        return b, h, i

    def output_tile(s):
        return score_tile(jnp.maximum(s - 2, 0))

    def q_index(s):
        b, h, i = score_tile(s)
        return h, 0, b * n_q + i

    def k_index(s):
        b, h, _ = score_tile(s)
        return h, b, 0

    def v_index(s):
        b, h, _ = score_tile(jnp.maximum(s - 1, 0))
        return h, 0, b

    def out_index(s):
        b, h, i = output_tile(s)
        return h, 0, b * n_q + i

    scores = pltpu.VMEM((2, n_kb, tk, tq), _F32)
    col_stat = pltpu.VMEM((2, F32_SUBLANES, tq), _F32)
    acc = pltpu.VMEM((2, HEAD_WIDTH, tq), _F32)
    return pl.pallas_call(
        kern,
        grid=(n_tiles + 2,),
        in_specs=[
            pl.BlockSpec(memory_space=pltpu.SMEM),
            pl.BlockSpec((4, HEAD_DIM), lambda s: (0, 0)),
            pl.BlockSpec((HEAD_WIDTH, 1), lambda s: (0, 0)),
            pl.BlockSpec((1, HEAD_WIDTH, tq), q_index),
            pl.BlockSpec((1, seq, HEAD_WIDTH), k_index),
            pl.BlockSpec((1, HEAD_WIDTH, seq), v_index),
        ],
        out_specs=pl.BlockSpec((1, HEAD_WIDTH, tq), out_index),
        out_shape=jax.ShapeDtypeStruct((n_heads, HEAD_WIDTH, T), _BF16),
        scratch_shapes=[
            pltpu.VMEM((2, n_kb, tk, HEAD_WIDTH), _BF16),
            pltpu.VMEM((tk // tq, tk, tq), _F32),
            scores, scores, col_stat, col_stat,
            acc, acc, col_stat, col_stat,
        ],
        compiler_params=_params("arbitrary"),
        name="diff_attn",
    )(slopes, lam_params, gain_col, q_t, k, v_t)


def _merge_kernel(o_ref, pz_ref, pzp_ref, pzn_ref, gz_ref, x_ref, bg_ref, pw_ref, ps_ref,
                  wa_ref, wp_ref, wo_ref, g_ref, out_ref, ext_ref, yp_ref, *, tm, seq):
    D = x_ref.shape[1]
    gdim = pw_ref.shape[1]
    rows = tm + 2 * HALO
    t0 = (pl.program_id(0) * tm) % seq
    prev = pzp_ref[...].astype(_F32)
    nxt = pzn_ref[...].astype(_F32)
    ext_ref[0:HALO] = jnp.where(t0 == 0, jnp.zeros_like(prev), prev)
    ext_ref[HALO:HALO + tm] = pz_ref[...].astype(_F32)
    ext_ref[HALO + tm:rows] = jnp.where(t0 + tm == seq, jnp.zeros_like(nxt), nxt)

    def ahead(a, k):
        return pltpu.roll(a, rows - k, axis=0)

    def behind(a, k):
        return pltpu.roll(a, k, axis=0)

    tpos = t0 + lax.broadcasted_iota(jnp.int32, (tm, 1), 0)
    for g, w in enumerate(POOL_WINDOWS):
        half = w // 2
        cols = slice(g * gdim, (g + 1) * gdim)
        e = ext_ref[:, cols]
        run, n = e, 1
        while n < half:
            run = run + ahead(run, n)
            n *= 2
        if half % F32_SUBLANES == 0:
            win = run[HALO - half:HALO - half + tm] + run[HALO:HALO + tm]
        else:
            win = (behind(run, half) + run)[HALO:HALO + tm]
        cnt = (jnp.minimum(tpos + half, seq) - jnp.maximum(tpos - half, 0)).astype(_F32)
        mixed = win * (1.0 / cnt) - e[HALO:HALO + tm]
        yg = jnp.dot(mixed.astype(_BF16), pw_ref[g], preferred_element_type=_F32)
        yp_ref[:, cols] = (yg * ps_ref[:, cols]).astype(_BF16)

    o_t = o_ref[...].reshape(wa_ref.shape[0], tm)
    y_attn = lax.dot_general(o_t, wa_ref[...], (((0,), (0,)), ((), ())), preferred_element_type=_F32)
    y_pool = jnp.dot(yp_ref[...], wp_ref[...], preferred_element_type=_F32)
    gates = jax.nn.sigmoid(gz_ref[...].astype(_F32) + bg_ref[...])
    merged = gates[:, :D] * y_attn + gates[:, D:] * y_pool
    mo = jnp.dot(merged.astype(_BF16), wo_ref[...], preferred_element_type=_F32)
    out_ref[...] = x_ref[...] + _rms(mo, g_ref[...], NORM_EPS)


def _const_spec(shape):
    nd = len(shape)
    return pl.BlockSpec(shape, lambda i: (0,) * nd, pipeline_mode=pl.Buffered(1))


def _merge(o, pz, gz, x2, b_gate, pool_w, pool_scale, w_attn, w_pool, w_out, g_post, *, seq, tm):
    T, D = x2.shape
    aw = pz.shape[1]
    n_heads = o.shape[0]
    hb = tm // HALO
    n_hb = T // HALO
    return pl.pallas_call(
        functools.partial(_merge_kernel, tm=tm, seq=seq),
        grid=(T // tm,),
        in_specs=[
            pl.BlockSpec((n_heads, HEAD_WIDTH, tm), lambda i: (0, 0, i)),
            pl.BlockSpec((tm, aw), lambda i: (i, 0)),
            pl.BlockSpec((HALO, aw), lambda i: (jnp.maximum(i * hb - 1, 0), 0)),
            pl.BlockSpec((HALO, aw), lambda i: (jnp.minimum((i + 1) * hb, n_hb - 1), 0)),
            pl.BlockSpec((tm, 2 * D), lambda i: (i, 0)),
            pl.BlockSpec((tm, D), lambda i: (i, 0)),
            _const_spec(b_gate.shape),
            _const_spec(pool_w.shape),
            _const_spec(pool_scale.shape),
            _const_spec(w_attn.shape),
            _const_spec(w_pool.shape),
            _const_spec(w_out.shape),
            _const_spec(g_post.shape),
        ],
        out_specs=pl.BlockSpec((tm, D), lambda i: (i, 0)),
        out_shape=jax.ShapeDtypeStruct((T, D), _F32),
        scratch_shapes=[
            pltpu.VMEM((tm + 2 * HALO, aw), _F32),
            pltpu.VMEM((tm, aw), _BF16),
        ],
        compiler_params=_params("arbitrary"),
        name="merge",
    )(o, pz, pz, pz, gz, x2, b_gate, pool_w, pool_scale, w_attn, w_pool, w_out, g_post)


def _ffn_kernel(x_ref, xp_ref, xn_ref, g_ref, wg_ref, wv_ref, cwg_ref, cwv_ref, cbg_ref, cbv_ref,
                wd_ref, gp_ref, out_ref, h_ref, acc_ref, *, tm, seq):
    c = pl.program_id(1)
    rows = tm + 2 * HALO

    @pl.when(c == 0)
    def _():
        t0 = (pl.program_id(0) * tm) % seq
        g = g_ref[...]
        hp = _rms(xp_ref[...], g, NORM_EPS)
        hn = _rms(xn_ref[...], g, NORM_EPS)
        h_ref[0:HALO] = jnp.where(t0 == 0, jnp.zeros_like(hp), hp).astype(_BF16)
        h_ref[HALO:HALO + tm] = _rms(x_ref[...], g, NORM_EPS).astype(_BF16)
        h_ref[HALO + tm:rows] = jnp.where(t0 + tm == seq, jnp.zeros_like(hn), hn).astype(_BF16)
        acc_ref[...] = jnp.zeros_like(acc_ref)

    h = h_ref[...]

    def conv(w_ref, cw_ref, cb_ref):
        u = jnp.dot(h, w_ref[...], preferred_element_type=_F32)
        cw = cw_ref[...]
        u_prev = pltpu.roll(u, 1, axis=0)[HALO:HALO + tm]
        u_next = pltpu.roll(u, rows - 1, axis=0)[HALO:HALO + tm]
        return u_prev * cw[0:1] + u[HALO:HALO + tm] * cw[1:2] + u_next * cw[2:3] + cb_ref[...]

    gate = conv(wg_ref, cwg_ref, cbg_ref)
    val = conv(wv_ref, cwv_ref, cbv_ref)
    act = (jax.nn.gelu(gate) * val).astype(_BF16)
    acc_ref[...] += jnp.dot(act, wd_ref[...], preferred_element_type=_F32)

    @pl.when(c == pl.num_programs(1) - 1)
    def _():
        def store(rows, y):
            out_ref[rows] = x_ref[rows] + y

        _rms_rows(acc_ref, gp_ref[...], NORM_EPS, store)


def _ffn(x1, g_pre, w_up, conv_w, conv_b, w_down, g_post, *, seq, tm, tf):
    T, D = x1.shape
    d_ff = w_down.shape[0]
    n_c = d_ff // tf
    hb = tm // HALO
    n_hb = T // HALO
    return pl.pallas_call(
        functools.partial(_ffn_kernel, tm=tm, seq=seq),
        grid=(T // tm, n_c),
        in_specs=[
            pl.BlockSpec((tm, D), lambda i, c: (i, 0)),
            pl.BlockSpec((HALO, D), lambda i, c: (jnp.maximum(i * hb - 1, 0), 0)),
            pl.BlockSpec((HALO, D), lambda i, c: (jnp.minimum((i + 1) * hb, n_hb - 1), 0)),
            pl.BlockSpec((1, D), lambda i, c: (0, 0)),
            pl.BlockSpec((D, tf), lambda i, c: (0, c)),
            pl.BlockSpec((D, tf), lambda i, c: (0, n_c + c)),
            pl.BlockSpec((CONV_WIDTH, tf), lambda i, c: (0, c)),
            pl.BlockSpec((CONV_WIDTH, tf), lambda i, c: (0, n_c + c)),
            pl.BlockSpec((1, tf), lambda i, c: (0, c)),
            pl.BlockSpec((1, tf), lambda i, c: (0, n_c + c)),
            pl.BlockSpec((tf, D), lambda i, c: (c, 0)),
            pl.BlockSpec((1, D), lambda i, c: (0, 0)),
        ],
        out_specs=pl.BlockSpec((tm, D), lambda i, c: (i, 0)),
        out_shape=jax.ShapeDtypeStruct((T, D), _F32),
        scratch_shapes=[
            pltpu.VMEM((tm + 2 * HALO, D), _BF16),
            pltpu.VMEM((tm, D), _F32),
        ],
        compiler_params=_params("arbitrary", "arbitrary"),
        name="conv_ffn",
    )(x1, x1, x1, g_pre, w_up, w_up, conv_w, conv_w, conv_b, conv_b, w_down, g_post)


def _pick(n, pref):
    t = min(n, pref)
    while n % t:
        t //= 2
    return t


def kernel(x, g_pre_mix, w_in, b_gate, lambda_q1, lambda_k1, lambda_q2, lambda_k2, subln_gain,
           w_attn_branch, pool_w, pool_scale, w_pool_branch, w_out, g_post_mix, g_pre_ffn, w_up,
           conv_w, conv_b, w_down, g_post_ffn):
    B, S, D = x.shape
    depth = w_in.shape[0]
    aw = w_attn_branch.shape[1]
    n_heads = aw // HEAD_WIDTH
    d_ff = w_down.shape[1]
    T = B * S
    slopes_np = np.array([2.0 ** (-8.0 * (h + 1) / n_heads) for h in range(n_heads)], dtype=np.float32)
    assert np.all(np.frexp(slopes_np)[0] == 0.5)
    slopes = jnp.asarray(slopes_np)
    row = lambda a: a.reshape(1, -1).astype(_F32)

    x2 = x.reshape(T, D)
    for l in range(depth):
        lambda_init = 0.8 - 0.6 * math.exp(-0.3 * l)
        q_t, k, v_t, pz, gz = _in_proj(x2, row(g_pre_mix[l]), w_in[l].astype(_BF16),
                                       n_heads=n_heads, tm=_pick(T, 1024))
        lam_params = jnp.stack([lambda_q1[l], lambda_k1[l], lambda_q2[l], lambda_k2[l]]).astype(_F32)
        o = _attention(q_t, k, v_t, slopes, lam_params, subln_gain[l].reshape(-1, 1).astype(_F32),
                       batch=B, seq=S, tq=_pick(S, 256), tk=_pick(S, 512), lambda_init=lambda_init)
        x2 = _merge(o, pz, gz, x2, row(b_gate[l]), pool_w[l].astype(_BF16), row(pool_scale[l]),
                    w_attn_branch[l].astype(_BF16), w_pool_branch[l].astype(_BF16),
                    w_out[l].astype(_BF16), row(g_post_mix[l]), seq=S, tm=_pick(S, 256))
        x2 = _ffn(x2, row(g_pre_ffn[l]), w_up[l].astype(_BF16), conv_w[l].astype(_F32),
                  row(conv_b[l]), w_down[l].astype(_BF16), row(g_post_ffn[l]),
                  seq=S, tm=_pick(S, 512), tf=_pick(d_ff, 512))
    return x2.reshape(B, S, D)
```

```python
import functools
import math

import jax
import jax.numpy as jnp
import numpy as np
from jax import lax
from jax.experimental import pallas as pl
from jax.experimental.pallas import tpu as pltpu

HEAD_DIM = 64
HEAD_WIDTH = 2 * HEAD_DIM
POOL_WINDOWS = (2, 4, 8, 16)
CONV_WIDTH = 3
NORM_EPS = 1e-6
SUBLN_EPS = 1e-5
HALO = 16
F32_SUBLANES = 8
MXU_COLS = 256
POS_RADIX = 64
LOG2E = math.log2(math.e)
V7X_VMEM_LIMIT_BYTES = 56 * 1024 * 1024

_F32 = jnp.float32
_BF16 = jnp.bfloat16


def _rms(x, g, eps):
    return x * lax.rsqrt(jnp.mean(x * x, axis=-1, keepdims=True) + eps) * g


def _rms_rows(src, g, eps, store):
    for r0 in range(0, src.shape[0], F32_SUBLANES):
        rows = slice(r0, r0 + F32_SUBLANES)
        store(rows, _rms(src[rows], g, eps))


def _divmod(x, n):
    if n & (n - 1) == 0:
        return x >> (n.bit_length() - 1), x & (n - 1)
    return x // n, x % n


def _params(*sem):
    return pltpu.CompilerParams(dimension_semantics=sem, vmem_limit_bytes=V7X_VMEM_LIMIT_BYTES)


def _in_proj_kernel(x_ref, g_ref, w_ref, qt_ref, k_ref, vt_ref, pz_ref, gz_ref, h_ref, *, n_heads):
    j = pl.program_id(1)

    @pl.when(j == 0)
    def _():
        h_ref[...] = _rms(x_ref[...], g_ref[...], NORM_EPS).astype(_BF16)

    def project(store):
        h = h_ref[...]
        for lo in range(0, w_ref.shape[1], MXU_COLS):
            store(lo, jnp.dot(h, w_ref[:, lo:lo + MXU_COLS], preferred_element_type=_F32))

    def heads_of(lo, z):
        for k in range(MXU_COLS // HEAD_WIDTH):
            yield lo // HEAD_WIDTH + k, z[:, k * HEAD_WIDTH:(k + 1) * HEAD_WIDTH]

    def store_q(lo, z):
        for head, z_h in heads_of(lo, z):
            qt_ref[head] = (z_h * (HEAD_DIM ** -0.5)).T.astype(_BF16)

    def store_k(lo, z):
        for head, z_h in heads_of(lo, z):
            k_ref[head] = z_h.astype(_BF16)

    def store_v(lo, z):
        for head, z_h in heads_of(lo, z):
            vt_ref[head] = z_h.T.astype(_BF16)

    def store_pz(lo, z):
        pz_ref[:, lo:lo + MXU_COLS] = z.astype(_BF16)

    def store_gz(lo, z):
        gz_ref[:, lo:lo + MXU_COLS] = z.astype(_BF16)

    @pl.when(j == 0)
    def _():
        project(store_q)

    @pl.when(j == 1)
    def _():
        project(store_k)

    @pl.when(j == 2)
    def _():
        project(store_v)

    @pl.when(j == 3)
    def _():
        project(store_pz)

    @pl.when(j > 3)
    def _():
        project(store_gz)


def _in_proj(x2, g, w, *, n_heads, tm):
    T, D = x2.shape
    aw = n_heads * HEAD_WIDTH
    n_gate_blocks = 2 * D // aw
    grid = (T // tm, 4 + n_gate_blocks)
    return pl.pallas_call(
        functools.partial(_in_proj_kernel, n_heads=n_heads),
        grid=grid,
        in_specs=[
            pl.BlockSpec((tm, D), lambda i, j: (i, 0)),
            pl.BlockSpec((1, D), lambda i, j: (0, 0)),
            pl.BlockSpec((D, aw), lambda i, j: (0, j)),
        ],
        out_specs=[
            pl.BlockSpec((n_heads, HEAD_WIDTH, tm), lambda i, j: (0, 0, i)),
            pl.BlockSpec((n_heads, tm, HEAD_WIDTH), lambda i, j: (0, i, 0)),
            pl.BlockSpec((n_heads, HEAD_WIDTH, tm), lambda i, j: (0, 0, i)),
            pl.BlockSpec((tm, aw), lambda i, j: (i, 0)),
            pl.BlockSpec((tm, aw), lambda i, j: (i, jnp.maximum(j - 4, 0))),
        ],
        out_shape=[
            jax.ShapeDtypeStruct((n_heads, HEAD_WIDTH, T), _BF16),
            jax.ShapeDtypeStruct((n_heads, T, HEAD_WIDTH), _BF16),
            jax.ShapeDtypeStruct((n_heads, HEAD_WIDTH, T), _BF16),
            jax.ShapeDtypeStruct((T, aw), _BF16),
            jax.ShapeDtypeStruct((T, 2 * D), _BF16),
        ],
        scratch_shapes=[pltpu.VMEM((tm, D), _BF16)],
        compiler_params=_params("arbitrary", "arbitrary"),
        name="in_proj",
    )(x2, g, w)


def _pos_features(pos, idx, base, values):
    hi = (pos // POS_RADIX).astype(_F32)
    lo = (pos % POS_RADIX).astype(_F32)
    out = jnp.zeros(pos.shape, _F32)
    for n, val in enumerate(values(hi, lo)):
        out = jnp.where(idx == base + n, val, out)
    return out


def _attn_kernel(slopes_ref, lam_ref, gain_ref, q_ref, k_ref, vt_ref, o_ref,
                 kaug_ref, corr_ref, s_even_ref, s_odd_ref, m_even_ref, m_odd_ref,
                 acc_even_ref, acc_odd_ref, l_even_ref, l_odd_ref,
                 *, tq, tk, n_kb, n_q, n_heads, n_tiles, lambda_init):
    step = pl.program_id(0)
    tile = jnp.minimum(step, n_tiles - 1)
    head, qi = _divmod(tile, n_q)
    feat_base = (HEAD_DIM, 0)

    @pl.when(step == 0)
    def _():
        s_odd_ref[...] = jnp.zeros_like(s_odd_ref)
        m_odd_ref[...] = jnp.zeros_like(m_odd_ref)
        acc_odd_ref[...] = jnp.zeros_like(acc_odd_ref)
        l_odd_ref[...] = jnp.ones_like(l_odd_ref)

    @pl.when((qi == 0) & (step < n_tiles))
    def _():
        slope = slopes_ref[_divmod(head, n_heads)[1]]
        lane = lax.broadcasted_iota(jnp.int32, (tk, HEAD_WIDTH), 1)
        row = lax.broadcasted_iota(jnp.int32, (tk, HEAD_WIDTH), 0)
        feats, feats_per_block, own_half = [], [], []
        for c in range(2):
            feats.append(_pos_features(
                row, lane, feat_base[c],
                lambda hi, lo: (-POS_RADIX * slope, -slope, POS_RADIX * slope * hi, slope * lo)))
            feats_per_block.append(jnp.where(lane == feat_base[c] + 2, slope * tk, 0.0))
            own_half.append((lane >= c * HEAD_DIM) & (lane < (c + 1) * HEAD_DIM))
        for jb in range(n_kb):
            kb = k_ref[0, jb * tk:(jb + 1) * tk, :].astype(_F32)
            for c in range(2):
                kaug_ref[c, jb] = jnp.where(own_half[c], kb, feats[c]).astype(_BF16)
                feats[c] = feats[c] + feats_per_block[c]
        rel = (lax.broadcasted_iota(jnp.int32, (tk, tq), 0)
               - lax.broadcasted_iota(jnp.int32, (tk, tq), 1))
        for v in range(tk // tq):
            corr_ref[v] = (2.0 * slope) * jnp.maximum(rel - v * tq, 0).astype(_F32)

    q_t = q_ref[0]
    q_half = (q_t[:HEAD_DIM], q_t[HEAD_DIM:])
    sub = lax.broadcasted_iota(jnp.int32, (HALO, tq), 0)
    qfeat = _pos_features(qi * tq + lax.broadcasted_iota(jnp.int32, (HALO, tq), 1),
                          sub, 0, lambda hi, lo: (hi, lo, 1.0, 1.0)).astype(_BF16)
    no_feat = jnp.zeros((HEAD_DIM - HALO, tq), _BF16)

    def with_features(c, f):
        other_half = jnp.concatenate([f, no_feat], axis=0)
        parts = (q_half[0], other_half) if c == 0 else (other_half, q_half[1])
        return jnp.concatenate(parts, axis=0)

    q_left = [with_features(c, qfeat) for c in range(2)]
    q_right = [with_features(c, -qfeat) for c in range(2)]

    jd, offset_in_block = _divmod(qi * tq, tk)
    correction = corr_ref[_divmod(offset_in_block, tq)[0]]

    lam_p = lam_ref[...]
    lam = (jnp.exp(jnp.sum(lam_p[0:1] * lam_p[1:2], axis=-1, keepdims=True))
           - jnp.exp(jnp.sum(lam_p[2:3] * lam_p[3:4], axis=-1, keepdims=True))
           + lambda_init)

    def run_step(s_new_ref, m_new_ref, acc_new_ref, l_new_ref,
                 s_old_ref, m_old_ref, acc_old_ref, l_old_ref):
        l_old = [jnp.sum(l_old_ref[c], axis=0, keepdims=True) for c in range(2)]
        o_t = acc_old_ref[0] / l_old[0] - lam * (acc_old_ref[1] / l_old[1])
        ms = jnp.mean(o_t * o_t, axis=0, keepdims=True)
        o_t = o_t * lax.rsqrt(ms + SUBLN_EPS) * gain_ref[...] * (1.0 - lambda_init)
        o_ref[0] = o_t.astype(o_ref.dtype)

        def score_block(d, col_max):
            jb = jd if d == 0 else _divmod(jd + d, n_kb)[1]
            out = []
            for c in range(2):
                q_c = q_left[c] if d == 0 else jnp.where(jb < jd, q_left[c], q_right[c])
                s = jnp.dot(kaug_ref[c, jb], q_c, preferred_element_type=_F32)
                if d == 0:
                    s = s - correction
                s = s * LOG2E
                s_new_ref[c, jb] = s
                block_max = jnp.max(s.reshape(tk // F32_SUBLANES, F32_SUBLANES, tq), axis=0)
                out.append(block_max if col_max is None else jnp.maximum(col_max[c], block_max))
            return out

        m_old = [jnp.max(m_old_ref[c], axis=0, keepdims=True) for c in range(2)]
        col_sum = [jnp.zeros((F32_SUBLANES, tq), _F32), jnp.zeros((F32_SUBLANES, tq), _F32)]
        col_max = None
        for jb in range(n_kb):
            col_max = score_block(jb, col_max)
            v_t = vt_ref[0, :, jb * tk:(jb + 1) * tk]
            for c in range(2):
                p = jnp.exp2(s_old_ref[c, jb] - m_old[c])
                col_sum[c] = col_sum[c] + jnp.sum(
                    p.reshape(tk // F32_SUBLANES, F32_SUBLANES, tq), axis=0)
                pv = jnp.dot(v_t, p.astype(_BF16), preferred_element_type=_F32)
                if jb == 0:
                    acc_new_ref[c] = pv
                else:
                    acc_new_ref[c] += pv
        for c in range(2):
            m_new_ref[c] = col_max[c]
            l_new_ref[c] = col_sum[c]

    even = (s_even_ref, m_even_ref, acc_even_ref, l_even_ref)
    odd = (s_odd_ref, m_odd_ref, acc_odd_ref, l_odd_ref)

    @pl.when(step % 2 == 0)
    def _():
        run_step(*even, *odd)

    @pl.when(step % 2 == 1)
    def _():
        run_step(*odd, *even)


def _attention(q_t, k, v_t, slopes, lam_params, gain_col, *, batch, seq, tq, tk, lambda_init):
    n_heads, T, _ = k.shape
    n_q = seq // tq
    n_kb = seq // tk
    n_tiles = batch * n_heads * n_q
    assert tk % tq == 0 and seq <= POS_RADIX * 256
    kern = functools.partial(_attn_kernel, tq=tq, tk=tk, n_kb=n_kb, n_q=n_q, n_heads=n_heads,
                             n_tiles=n_tiles, lambda_init=lambda_init)

    def score_tile(s):
        t = jnp.minimum(s, n_tiles - 1)
        head, i = _divmod(t, n_q)
        b, h = _divmod(head, n_heads)
        return b, h, i

    def output_tile(s):
        return score_tile(jnp.maximum(s - 2, 0))

    def q_index(s):
        b, h, i = score_tile(s)
        return h, 0, b * n_q + i

    def k_index(s):
        b, h, _ = score_tile(s)
        return h, b, 0

    def v_index(s):
        b, h, _ = score_tile(jnp.maximum(s - 1, 0))
        return h, 0, b

    def out_index(s):
        b, h, i = output_tile(s)
        return h, 0, b * n_q + i

    scores = pltpu.VMEM((2, n_kb, tk, tq), _F32)
    col_stat = pltpu.VMEM((2, F32_SUBLANES, tq), _F32)
    acc = pltpu.VMEM((2, HEAD_WIDTH, tq), _F32)
    return pl.pallas_call(
        kern,
        grid=(n_tiles + 2,),
        in_specs=[
            pl.BlockSpec(memory_space=pltpu.SMEM),
            pl.BlockSpec((4, HEAD_DIM), lambda s: (0, 0)),
            pl.BlockSpec((HEAD_WIDTH, 1), lambda s: (0, 0)),
            pl.BlockSpec((1, HEAD_WIDTH, tq), q_index),
            pl.BlockSpec((1, seq, HEAD_WIDTH), k_index),
            pl.BlockSpec((1, HEAD_WIDTH, seq), v_index),
        ],
        out_specs=pl.BlockSpec((1, HEAD_WIDTH, tq), out_index),
        out_shape=jax.ShapeDtypeStruct((n_heads, HEAD_WIDTH, T), _BF16),
        scratch_shapes=[
            pltpu.VMEM((2, n_kb, tk, HEAD_WIDTH), _BF16),
            pltpu.VMEM((tk // tq, tk, tq), _F32),
            scores, scores, col_stat, col_stat,
            acc, acc, col_stat, col_stat,
        ],
        compiler_params=_params("arbitrary"),
        name="diff_attn",
    )(slopes, lam_params, gain_col, q_t, k, v_t)


def _merge_kernel(o_ref, pz_ref, pzp_ref, pzn_ref, gz_ref, x_ref, bg_ref, pw_ref, ps_ref,
                  wa_ref, wp_ref, wo_ref, g_ref, out_ref, ext_ref, yp_ref, *, tm, seq):
    D = x_ref.shape[1]
    gdim = pw_ref.shape[1]
    rows = tm + 2 * HALO
    t0 = (pl.program_id(0) * tm) % seq
    prev = pzp_ref[...].astype(_F32)
    nxt = pzn_ref[...].astype(_F32)
    ext_ref[0:HALO] = jnp.where(t0 == 0, jnp.zeros_like(prev), prev)
    ext_ref[HALO:HALO + tm] = pz_ref[...].astype(_F32)
    ext_ref[HALO + tm:rows] = jnp.where(t0 + tm == seq, jnp.zeros_like(nxt), nxt)

    def ahead(a, k):
        return pltpu.roll(a, rows - k, axis=0)

    def behind(a, k):
        return pltpu.roll(a, k, axis=0)

    tpos = t0 + lax.broadcasted_iota(jnp.int32, (tm, 1), 0)
    for g, w in enumerate(POOL_WINDOWS):
        half = w // 2
        cols = slice(g * gdim, (g + 1) * gdim)
        e = ext_ref[:, cols]
        run, n = e, 1
        while n < half:
            run = run + ahead(run, n)
            n *= 2
        if half % F32_SUBLANES == 0:
            win = run[HALO - half:HALO - half + tm] + run[HALO:HALO + tm]
        else:
            win = (behind(run, half) + run)[HALO:HALO + tm]
        cnt = (jnp.minimum(tpos + half, seq) - jnp.maximum(tpos - half, 0)).astype(_F32)
        mixed = win * (1.0 / cnt) - e[HALO:HALO + tm]
        yg = jnp.dot(mixed.astype(_BF16), pw_ref[g], preferred_element_type=_F32)
        yp_ref[:, cols] = (yg * ps_ref[:, cols]).astype(_BF16)

    o_t = o_ref[...].reshape(wa_ref.shape[0], tm)
    y_attn = lax.dot_general(o_t, wa_ref[...], (((0,), (0,)), ((), ())), preferred_element_type=_F32)
    y_pool = jnp.dot(yp_ref[...], wp_ref[...], preferred_element_type=_F32)
    gates = jax.nn.sigmoid(gz_ref[...].astype(_F32) + bg_ref[...])
    merged = gates[:, :D] * y_attn + gates[:, D:] * y_pool
    mo = jnp.dot(merged.astype(_BF16), wo_ref[...], preferred_element_type=_F32)
    out_ref[...] = x_ref[...] + _rms(mo, g_ref[...], NORM_EPS)


def _const_spec(shape):
    nd = len(shape)
    return pl.BlockSpec(shape, lambda i: (0,) * nd, pipeline_mode=pl.Buffered(1))


def _merge(o, pz, gz, x2, b_gate, pool_w, pool_scale, w_attn, w_pool, w_out, g_post, *, seq, tm):
    T, D = x2.shape
    aw = pz.shape[1]
    n_heads = o.shape[0]
    hb = tm // HALO
    n_hb = T // HALO
    return pl.pallas_call(
        functools.partial(_merge_kernel, tm=tm, seq=seq),
        grid=(T // tm,),
        in_specs=[
            pl.BlockSpec((n_heads, HEAD_WIDTH, tm), lambda i: (0, 0, i)),
            pl.BlockSpec((tm, aw), lambda i: (i, 0)),
            pl.BlockSpec((HALO, aw), lambda i: (jnp.maximum(i * hb - 1, 0), 0)),
            pl.BlockSpec((HALO, aw), lambda i: (jnp.minimum((i + 1) * hb, n_hb - 1), 0)),
            pl.BlockSpec((tm, 2 * D), lambda i: (i, 0)),
            pl.BlockSpec((tm, D), lambda i: (i, 0)),
            _const_spec(b_gate.shape),
            _const_spec(pool_w.shape),
            _const_spec(pool_scale.shape),
            _const_spec(w_attn.shape),
            _const_spec(w_pool.shape),
            _const_spec(w_out.shape),
            _const_spec(g_post.shape),
        ],
        out_specs=pl.BlockSpec((tm, D), lambda i: (i, 0)),
        out_shape=jax.ShapeDtypeStruct((T, D), _F32),
        scratch_shapes=[
            pltpu.VMEM((tm + 2 * HALO, aw), _F32),
            pltpu.VMEM((tm, aw), _BF16),
        ],
        compiler_params=_params("arbitrary"),
        name="merge",
    )(o, pz, pz, pz, gz, x2, b_gate, pool_w, pool_scale, w_attn, w_pool, w_out, g_post)


def _ffn_kernel(x_ref, xp_ref, xn_ref, g_ref, wg_ref, wv_ref, cwg_ref, cwv_ref, cbg_ref, cbv_ref,
                wd_ref, gp_ref, out_ref, h_ref, acc_ref, *, tm, seq):
    c = pl.program_id(1)
    rows = tm + 2 * HALO

    @pl.when(c == 0)
    def _():
        t0 = (pl.program_id(0) * tm) % seq
        g = g_ref[...]
        hp = _rms(xp_ref[...], g, NORM_EPS)
        hn = _rms(xn_ref[...], g, NORM_EPS)
        h_ref[0:HALO] = jnp.where(t0 == 0, jnp.zeros_like(hp), hp).astype(_BF16)
        h_ref[HALO:HALO + tm] = _rms(x_ref[...], g, NORM_EPS).astype(_BF16)
        h_ref[HALO + tm:rows] = jnp.where(t0 + tm == seq, jnp.zeros_like(hn), hn).astype(_BF16)
        acc_ref[...] = jnp.zeros_like(acc_ref)

    h = h_ref[...]

    def conv(w_ref, cw_ref, cb_ref):
        u = jnp.dot(h, w_ref[...], preferred_element_type=_F32)
        cw = cw_ref[...]
        u_prev = pltpu.roll(u, 1, axis=0)[HALO:HALO + tm]
        u_next = pltpu.roll(u, rows - 1, axis=0)[HALO:HALO + tm]
        return u_prev * cw[0:1] + u[HALO:HALO + tm] * cw[1:2] + u_next * cw[2:3] + cb_ref[...]

    gate = conv(wg_ref, cwg_ref, cbg_ref)
    val = conv(wv_ref, cwv_ref, cbv_ref)
    act = (jax.nn.gelu(gate) * val).astype(_BF16)
    acc_ref[...] += jnp.dot(act, wd_ref[...], preferred_element_type=_F32)

    @pl.when(c == pl.num_programs(1) - 1)
    def _():
        def store(rows, y):
            out_ref[rows] = x_ref[rows] + y

        _rms_rows(acc_ref, gp_ref[...], NORM_EPS, store)


def _ffn(x1, g_pre, w_up, conv_w, conv_b, w_down, g_post, *, seq, tm, tf):
    T, D = x1.shape
    d_ff = w_down.shape[0]
    n_c = d_ff // tf
    hb = tm // HALO
    n_hb = T // HALO
    return pl.pallas_call(
        functools.partial(_ffn_kernel, tm=tm, seq=seq),
        grid=(T // tm, n_c),
        in_specs=[
            pl.BlockSpec((tm, D), lambda i, c: (i, 0)),
            pl.BlockSpec((HALO, D), lambda i, c: (jnp.maximum(i * hb - 1, 0), 0)),
            pl.BlockSpec((HALO, D), lambda i, c: (jnp.minimum((i + 1) * hb, n_hb - 1), 0)),
            pl.BlockSpec((1, D), lambda i, c: (0, 0)),
            pl.BlockSpec((D, tf), lambda i, c: (0, c)),
            pl.BlockSpec((D, tf), lambda i, c: (0, n_c + c)),
            pl.BlockSpec((CONV_WIDTH, tf), lambda i, c: (0, c)),
            pl.BlockSpec((CONV_WIDTH, tf), lambda i, c: (0, n_c + c)),
            pl.BlockSpec((1, tf), lambda i, c: (0, c)),
            pl.BlockSpec((1, tf), lambda i, c: (0, n_c + c)),
            pl.BlockSpec((tf, D), lambda i, c: (c, 0)),
            pl.BlockSpec((1, D), lambda i, c: (0, 0)),
        ],
        out_specs=pl.BlockSpec((tm, D), lambda i, c: (i, 0)),
        out_shape=jax.ShapeDtypeStruct((T, D), _F32),
        scratch_shapes=[
            pltpu.VMEM((tm + 2 * HALO, D), _BF16),
            pltpu.VMEM((tm, D), _F32),
        ],
        compiler_params=_params("arbitrary", "arbitrary"),
        name="conv_ffn",
    )(x1, x1, x1, g_pre, w_up, w_up, conv_w, conv_w, conv_b, conv_b, w_down, g_post)


def _pick(n, pref):
    t = min(n, pref)
    while n % t:
        t //= 2
    return t


def kernel(x, g_pre_mix, w_in, b_gate, lambda_q1, lambda_k1, lambda_q2, lambda_k2, subln_gain,
           w_attn_branch, pool_w, pool_scale, w_pool_branch, w_out, g_post_mix, g_pre_ffn, w_up,
           conv_w, conv_b, w_down, g_post_ffn):
    B, S, D = x.shape
    depth = w_in.shape[0]
    aw = w_attn_branch.shape[1]
    n_heads = aw // HEAD_WIDTH
    d_ff = w_down.shape[1]
    T = B * S
    slopes_np = np.array([2.0 ** (-8.0 * (h + 1) / n_heads) for h in range(n_heads)], dtype=np.float32)
    assert np.all(np.frexp(slopes_np)[0] == 0.5)
    slopes = jnp.asarray(slopes_np)
    row = lambda a: a.reshape(1, -1).astype(_F32)

    x2 = x.reshape(T, D)
    for l in range(depth):
        lambda_init = 0.8 - 0.6 * math.exp(-0.3 * l)
        q_t, k, v_t, pz, gz = _in_proj(x2, row(g_pre_mix[l]), w_in[l].astype(_BF16),
                                       n_heads=n_heads, tm=_pick(T, 1024))
        lam_params = jnp.stack([lambda_q1[l], lambda_k1[l], lambda_q2[l], lambda_k2[l]]).astype(_F32)
        o = _attention(q_t, k, v_t, slopes, lam_params, subln_gain[l].reshape(-1, 1).astype(_F32),
                       batch=B, seq=S, tq=_pick(S, 256), tk=_pick(S, 512), lambda_init=lambda_init)
        x2 = _merge(o, pz, gz, x2, row(b_gate[l]), pool_w[l].astype(_BF16), row(pool_scale[l]),
                    w_attn_branch[l].astype(_BF16), w_pool_branch[l].astype(_BF16),
                    w_out[l].astype(_BF16), row(g_post_mix[l]), seq=S, tm=_pick(S, 512))
        x2 = _ffn(x2, row(g_pre_ffn[l]), w_up[l].astype(_BF16), conv_w[l].astype(_F32),
                  row(conv_b[l]), w_down[l].astype(_BF16), row(g_post_ffn[l]),
                  seq=S, tm=_pick(S, 512), tf=_pick(d_ff, 512))
    return x2.reshape(B, S, D)
```
